```python
import jax, jax.numpy as jnp
from jax import lax
import numpy as np

D_MODEL = 2048
BATCH = 8
SEQ = 2048
DEPTH = 1
DEC_BATCH = 128
DEC_SEQ = 4
PAST_LEN = 2048
PAGE_SIZE = 128

H_A = 8
HD_A = 128
W_A = H_A * HD_A
H_B = 4
DK_B = 128
DV_B = 256
WK_B = H_B * DK_B
WV_B = H_B * DV_B
GK_RANK = 16
GK_NORM = 16.0
GLA_CHUNK = 64
MIX = W_A + WV_B
IN_COLS = 3 * W_A + H_A + 2 * WK_B + 2 * WV_B + GK_RANK
Q_BLOCK = 128
D_FF = 4 * D_MODEL
PLE_DIM = 256
EPS = 1e-6
NEG = -1e30
POOL_NUM = 5
POOL_DEN = 4

kernel_name = "hymba_fox_gla_decode_step"


def rmsnorm(x, g):
    xf = x.astype(jnp.float32)
    y = xf * lax.rsqrt(jnp.mean(xf * xf, axis=-1, keepdims=True) + EPS)
    return (y * g.astype(jnp.float32)).astype(x.dtype)


def split_proj(h, w_in, b_f, w_gk2, b_gk):
    B, T, _ = h.shape
    z = h @ w_in
    sizes = (W_A, W_A, W_A, H_A, WK_B, WK_B, WV_B, WV_B, GK_RANK)
    idx = []
    acc = 0
    for s in sizes[:-1]:
        acc += s
        idx.append(acc)
    q_a, k_a, v_a, f_a, q_b, k_b, v_b, r_b, g_lr = jnp.split(z, idx, axis=-1)
    q_a = q_a.reshape(B, T, H_A, HD_A)
    k_a = k_a.reshape(B, T, H_A, HD_A)
    v_a = v_a.reshape(B, T, H_A, HD_A)
    logf = jax.nn.log_sigmoid((f_a + b_f).astype(jnp.float32))
    q_b = q_b.reshape(B, T, H_B, DK_B) * (DK_B ** -0.5)
    k_b = k_b.reshape(B, T, H_B, DK_B)
    v_b = v_b.reshape(B, T, H_B, DV_B)
    r_b = r_b.reshape(B, T, H_B, DV_B)
    log_a = jax.nn.log_sigmoid((g_lr @ w_gk2 + b_gk).astype(jnp.float32)) / GK_NORM
    log_a = log_a.reshape(B, T, H_B, DK_B)
    return q_a, k_a, v_a, logf, q_b, k_b, v_b, r_b, log_a


def fox_block(q, c_q, q_pos, k, v, c_k, k_pos):
    s = jnp.einsum('bqhd,bkhd->bhqk', q, k).astype(jnp.float32) * (HD_A ** -0.5)
    s = s + jnp.transpose(c_q, (0, 2, 1))[:, :, :, None] - jnp.transpose(c_k, (0, 2, 1))[:, :, None, :]
    mask = k_pos[None, :] <= q_pos[:, None]
    s = jnp.where(mask[None, None], s, NEG)
    p = jax.nn.softmax(s, axis=-1).astype(v.dtype)
    return jnp.einsum('bhqk,bkhd->bqhd', p, v)


def fox_prompt(q, k, v, logf):
    B, T, H, D = q.shape
    c = jnp.cumsum(logf, axis=1)
    nb = T // Q_BLOCK
    qb = jnp.moveaxis(q.reshape(B, nb, Q_BLOCK, H, D), 1, 0)
    cb = jnp.moveaxis(c.reshape(B, nb, Q_BLOCK, H), 1, 0)
    k_pos = jnp.arange(T)

    def one(args):
        qi, ci, i = args
        q_pos = i * Q_BLOCK + jnp.arange(Q_BLOCK)
        return fox_block(qi, ci, q_pos, k, v, c, k_pos)

    o = lax.map(one, (qb, cb, jnp.arange(nb)))
    return jnp.moveaxis(o, 0, 1).reshape(B, T, H, D)


def gla_chunked(q, k, v, log_a, S0):
    B, T, H, DK = q.shape
    DV = v.shape[-1]
    chunk = GLA_CHUNK if T % GLA_CHUNK == 0 else T
    n = T // chunk

    def to_chunks(a):
        return jnp.moveaxis(a.astype(jnp.float32).reshape((B, n, chunk) + a.shape[2:]), 1, 0)

    causal = jnp.tril(jnp.ones((chunk, chunk), dtype=bool))

    def step(S, inp):
        qc, kc, vc, ac = inp
        b = jnp.cumsum(ac, axis=1)
        b_last = b[:, -1]
        qe = qc * jnp.exp(b)
        ke = kc * jnp.exp(-b)
        kd = kc * jnp.exp(b_last[:, None] - b)
        o_inter = jnp.einsum('bchk,bhkv->bchv', qe, S)
        A = jnp.einsum('bchk,bshk->bhcs', qe, ke)
        A = jnp.where(causal[None, None], A, 0.0)
        o_intra = jnp.einsum('bhcs,bshv->bchv', A, vc)
        S_new = S * jnp.exp(b_last)[..., None] + jnp.einsum('bchk,bchv->bhkv', kd, vc)
        return S_new, o_inter + o_intra

    S_T, o = lax.scan(step, S0.astype(jnp.float32), (to_chunks(q), to_chunks(k), to_chunks(v), to_chunks(log_a)))
    o = jnp.moveaxis(o, 0, 1).reshape(B, T, H, DV).astype(v.dtype)
    return S_T, o


def mix_out(o_a, o_b, r_b, g_gla_out, w_out):
    B, T = o_a.shape[:2]
    o_b = rmsnorm(o_b, g_gla_out) * jax.nn.silu(r_b)
    o = jnp.concatenate([o_a.reshape(B, T, W_A), o_b.reshape(B, T, WV_B)], axis=-1)
    return o @ w_out


def channel_and_ple(x, p, g_mlp, w_up, w_down, w_ple, g_ple, g_ple_gate, w_ple_gate):
    h = rmsnorm(x, g_mlp)
    x = x + jnp.square(jax.nn.relu(h @ w_up)) @ w_down
    e = rmsnorm(p @ w_ple, g_ple)
    gate = jax.nn.sigmoid(rmsnorm(x, g_ple_gate) @ w_ple_gate)
    return x + gate * e


def setup_inputs(seed: int = 0) -> dict:
    key = jax.random.key(seed)
    ks = jax.random.split(key, 32)
    n_pages = PAST_LEN // PAGE_SIZE
    n_pool = (DEC_BATCH * n_pages * POOL_NUM) // POOL_DEN

    def nrm(k, shape, scale=1.0):
        return jax.random.normal(k, shape, jnp.float32) * scale

    page_table = jax.random.permutation(ks[6], n_pool)[: DEC_BATCH * n_pages]
    page_table = page_table.reshape(DEC_BATCH, n_pages).astype(jnp.int32)
    return {
        "x_prompt": nrm(ks[0], (BATCH, SEQ, D_MODEL)),
        "x_sample": nrm(ks[1], (DEC_BATCH, DEC_SEQ, D_MODEL)),
        "cache_k": nrm(ks[2], (DEPTH, n_pool, PAGE_SIZE, H_A, HD_A)),
        "cache_v": nrm(ks[3], (DEPTH, n_pool, PAGE_SIZE, H_A, HD_A)),
        "cache_logf": jax.nn.log_sigmoid(2.0 + nrm(ks[4], (DEPTH, n_pool, PAGE_SIZE, H_A))),
        "state_gla": nrm(ks[5], (DEPTH, DEC_BATCH, H_B, DK_B, DV_B), 0.5),
        "page_table": page_table,
        "p_prompt": nrm(ks[7], (DEPTH, BATCH, SEQ, PLE_DIM)),
        "p_sample": nrm(ks[8], (DEPTH, DEC_BATCH, DEC_SEQ, PLE_DIM)),
        "g_mix": 1.0 + nrm(ks[9], (DEPTH, D_MODEL), 0.02),
        "w_in": nrm(ks[10], (DEPTH, D_MODEL, IN_COLS), D_MODEL ** -0.5),
        "b_f": 2.0 + nrm(ks[11], (DEPTH, H_A), 0.1),
        "w_gk2": nrm(ks[12], (DEPTH, GK_RANK, WK_B), GK_RANK ** -0.5),
        "b_gk": nrm(ks[13], (DEPTH, WK_B), 0.1),
        "g_gla_out": 1.0 + nrm(ks[14], (DEPTH, DV_B), 0.02),
        "w_out": nrm(ks[15], (DEPTH, MIX, D_MODEL), MIX ** -0.5),
        "g_mlp": 1.0 + nrm(ks[16], (DEPTH, D_MODEL), 0.02),
        "w_up": nrm(ks[17], (DEPTH, D_MODEL, D_FF), D_MODEL ** -0.5),
        "w_down": nrm(ks[18], (DEPTH, D_FF, D_MODEL), D_FF ** -0.5),
        "w_ple": nrm(ks[19], (DEPTH, PLE_DIM, D_MODEL), PLE_DIM ** -0.5),
        "g_ple": 1.0 + nrm(ks[20], (DEPTH, D_MODEL), 0.02),
        "g_ple_gate": 1.0 + nrm(ks[21], (DEPTH, D_MODEL), 0.02),
        "w_ple_gate": nrm(ks[22], (DEPTH, D_MODEL, D_MODEL), D_MODEL ** -0.5),
        "g_final": 1.0 + nrm(ks[23], (D_MODEL,), 0.02),
    }


def reference(x_prompt, x_sample, cache_k, cache_v, cache_logf, state_gla, page_table,
              p_prompt, p_sample, g_mix, w_in, b_f, w_gk2, b_gk, g_gla_out, w_out,
              g_mlp, w_up, w_down, w_ple, g_ple, g_ple_gate, w_ple_gate, g_final):
    n_pages = page_table.shape[1]
    past = n_pages * PAGE_SIZE
    db, t_s = x_sample.shape[:2]
    xp = x_prompt
    xs = x_sample
    kp_l, vp_l, fp_l, sp_l = [], [], [], []
    ks_l, vs_l, fs_l, ss_l = [], [], [], []
    for l in range(DEPTH):
        h = rmsnorm(xp, g_mix[l])
        q_a, k_a, v_a, logf, q_b, k_b, v_b, r_b, log_a = split_proj(h, w_in[l], b_f[l], w_gk2[l], b_gk[l])
        o_a = fox_prompt(q_a, k_a, v_a, logf)
        S0 = jnp.zeros((xp.shape[0], H_B, DK_B, DV_B), jnp.float32)
        S_p, o_b = gla_chunked(q_b, k_b, v_b, log_a, S0)
        xp = xp + mix_out(o_a, o_b, r_b, g_gla_out[l], w_out[l])
        xp = channel_and_ple(xp, p_prompt[l], g_mlp[l], w_up[l], w_down[l], w_ple[l], g_ple[l], g_ple_gate[l], w_ple_gate[l])
        kp_l.append(k_a)
        vp_l.append(v_a)
        fp_l.append(logf)
        sp_l.append(S_p)

        h = rmsnorm(xs, g_mix[l])
        q_a, k_a, v_a, logf, q_b, k_b, v_b, r_b, log_a = split_proj(h, w_in[l], b_f[l], w_gk2[l], b_gk[l])
        k_past = cache_k[l][page_table].reshape(db, past, H_A, HD_A)
        v_past = cache_v[l][page_table].reshape(db, past, H_A, HD_A)
        f_past = cache_logf[l][page_table].reshape(db, past, H_A).astype(jnp.float32)
        k_all = jnp.concatenate([k_past, k_a.astype(k_past.dtype)], axis=1)
        v_all = jnp.concatenate([v_past, v_a.astype(v_past.dtype)], axis=1)
        c_all = jnp.cumsum(jnp.concatenate([f_past, logf], axis=1), axis=1)
        q_pos = past + jnp.arange(t_s)
        k_pos = jnp.arange(past + t_s)
        o_a = fox_block(q_a, c_all[:, past:], q_pos, k_all, v_all, c_all, k_pos).astype(xs.dtype)
        S_s, o_b = gla_chunked(q_b, k_b, v_b, log_a, state_gla[l])
        xs = xs + mix_out(o_a, o_b, r_b, g_gla_out[l], w_out[l])
        xs = channel_and_ple(xs, p_sample[l], g_mlp[l], w_up[l], w_down[l], w_ple[l], g_ple[l], g_ple_gate[l], w_ple_gate[l])
        ks_l.append(k_a)
        vs_l.append(v_a)
        fs_l.append(logf)
        ss_l.append(S_s)

    y_prompt = rmsnorm(xp, g_final)
    y_sample = rmsnorm(xs, g_final)
    return (y_prompt, y_sample,
            jnp.stack(kp_l), jnp.stack(vp_l), jnp.stack(fp_l), jnp.stack(sp_l),
            jnp.stack(ks_l), jnp.stack(vs_l), jnp.stack(fs_l), jnp.stack(ss_l))
```

```python
import functools

import jax
import jax.numpy as jnp
from jax import lax
from jax.experimental import pallas as pl
from jax.experimental.pallas import tpu as pltpu

f32 = jnp.float32
bf16 = jnp.bfloat16

EPS = 1e-6
NEG = -1e30
GK_NORM = 16.0
GLA_CHUNK = 64

LANE = 128
SUBLANE = 8
MIB = 2**20

_NT = (((1,), (1,)), ((), ()))
_TN = (((0,), (0,)), ((), ()))


def _params(semantics, vmem_mib):
    return pltpu.CompilerParams(dimension_semantics=semantics, vmem_limit_bytes=vmem_mib * MIB)


def _resident(shape):
    return pl.BlockSpec(shape, lambda *_: (0,) * len(shape), pipeline_mode=pl.Buffered(1))


def _rms(x, g):
    return x * lax.rsqrt(jnp.mean(x * x, axis=-1, keepdims=True) + EPS) * g


def _log_sigmoid(x):
    return jnp.minimum(x, 0.0) - jnp.log1p(jnp.exp(-jnp.abs(x)))


def _split3(x):
    hi = x.astype(bf16)
    r = x - hi.astype(f32)
    mid = r.astype(bf16)
    lo = (r - mid.astype(f32)).astype(bf16)
    return hi, mid, lo


def _dot(a, b):
    return jnp.dot(a, b, preferred_element_type=f32)


def _dot_sel_left(sel, x):
    return sum(_dot(sel, t) for t in _split3(x))


def _tri(n):
    r = lax.broadcasted_iota(jnp.int32, (n, n), 0)
    c = lax.broadcasted_iota(jnp.int32, (n, n), 1)
    return (r >= c).astype(bf16)


def _inproj_kernel(x_ref, g_ref, w_ref, wfg_ref,
                   qa_ref, kab_ref, vab_ref, qb_ref, kb_ref, vb_ref, rb_ref, ka_ref, va_ref, fg_ref,
                   *, w_a, wk_b, wv_b, hd, scale_a, scale_b, chunk):
    h = _rms(x_ref[...], g_ref[...]).astype(bf16)
    col = 0

    def section(width, store):
        nonlocal col
        for j in range(0, width, chunk):
            store(j, _dot(h, w_ref[:, col + j:col + j + chunk]))
        col += width

    def per_head(ref, j, z):
        for t in range(chunk // hd):
            ref[:, j // hd + t, :] = z[:, t * hd:(t + 1) * hd]

    def st_qa(j, z):
        qa_ref[:, j:j + chunk] = (z * scale_a).astype(bf16)

    def st_ka(j, z):
        per_head(ka_ref, j, z)
        kab_ref[:, j:j + chunk] = z.astype(bf16)

    def st_va(j, z):
        per_head(va_ref, j, z)
        vab_ref[:, j:j + chunk] = z.astype(bf16)

    def st_qb(j, z):
        qb_ref[:, j:j + chunk] = (z * scale_b).astype(bf16)

    def st_kb(j, z):
        kb_ref[:, j:j + chunk] = z.astype(bf16)

    def st_vb(j, z):
        vb_ref[:, j:j + chunk] = z.astype(bf16)

    def st_rb(j, z):
        rb_ref[:, j:j + chunk] = z.astype(bf16)

    section(w_a, st_qa)
    section(w_a, st_ka)
    section(w_a, st_va)
    section(wk_b, st_qb)
    section(wk_b, st_kb)
    section(wv_b, st_vb)
    section(wv_b, st_rb)
    fg_ref[...] = _dot(h, wfg_ref[...])


def _inproj(x, g, w_main, w_fg, *, nh, hd, wk_b, wv_b, scale_a, scale_b, tm):
    n, d = x.shape
    cm = w_main.shape[1]
    w_a = nh * hd
    row = lambda width: pl.BlockSpec((tm, width), lambda i: (i, 0))
    heads = pl.BlockSpec((tm, nh, hd), lambda i: (i, 0, 0))
    sds = jax.ShapeDtypeStruct
    kern = functools.partial(_inproj_kernel, w_a=w_a, wk_b=wk_b, wv_b=wv_b, hd=hd,
                             scale_a=scale_a, scale_b=scale_b, chunk=512)
    return pl.pallas_call(
        kern,
        grid=(n // tm,),
        in_specs=[row(d), _resident((1, d)), _resident((d, cm)), _resident((d, LANE))],
        out_specs=[row(w_a), row(w_a), row(w_a), row(wk_b), row(wk_b), row(wv_b), row(wv_b),
                   heads, heads, row(LANE)],
        out_shape=[sds((n, w_a), bf16), sds((n, w_a), bf16), sds((n, w_a), bf16),
                   sds((n, wk_b), bf16), sds((n, wk_b), bf16), sds((n, wv_b), bf16), sds((n, wv_b), bf16),
                   sds((n, nh, hd), f32), sds((n, nh, hd), f32), sds((n, LANE), f32)],
        compiler_params=_params(("parallel",), 52),
        name="inproj",
    )(x, g, w_main, w_fg)


def _fox_prep_kernel(fg_ref, bf_ref, logf_ref, c_ref, ct_ref, *, nh, blk):
    t = fg_ref.shape[0]
    tri = _tri(blk)
    carry = jnp.zeros((1, LANE), f32)
    for b in range(t // blk):
        rows = slice(b * blk, (b + 1) * blk)
        logf = _log_sigmoid(fg_ref[rows, :] + bf_ref[...])
        logf_ref[rows, :] = logf[:, :nh]
        c = _dot_sel_left(tri, logf) + carry
        carry = c[blk - 1:blk, :]
        c_ref[rows, :] = c
        ct_ref[0, :, rows] = c.T[:nh, :]


def _fox_prep(fg, bf_pad, *, batch, nh):
    n = fg.shape[0]
    t = n // batch
    blk = min(t, 256)
    sds = jax.ShapeDtypeStruct
    return pl.pallas_call(
        functools.partial(_fox_prep_kernel, nh=nh, blk=blk),
        grid=(batch,),
        in_specs=[pl.BlockSpec((t, LANE), lambda b: (b, 0)), _resident((1, LANE))],
        out_specs=[pl.BlockSpec((t, nh), lambda b: (b, 0)),
                   pl.BlockSpec((t, LANE), lambda b: (b, 0)),
                   pl.BlockSpec((1, nh, t), lambda b: (b, 0, 0))],
        out_shape=[sds((n, nh), f32), sds((n, LANE), f32), sds((batch, nh, t), f32)],
        compiler_params=_params(("parallel",), 32),
        name="fox_prep",
    )(fg, bf_pad)


def _fox_attn_kernel(q_ref, k_ref, v_ref, c_ref, ct_ref, o_ref, *, nh, hd, tq):
    i = pl.program_id(1)
    row = lax.broadcasted_iota(jnp.int32, (tq, tq), 0)
    col = lax.broadcasted_iota(jnp.int32, (tq, tq), 1)
    causal = col <= row

    for h in range(nh):
        lanes = slice(h * hd, (h + 1) * hd)
        q = q_ref[:, lanes]
        cq = c_ref[:, h:h + 1]

        def block(j, carry, diag, lanes=lanes, q=q, cq=cq, h=h):
            m, l, acc = carry
            start = pl.multiple_of(j * tq, tq)
            kb = k_ref[pl.ds(start, tq), lanes]
            vb = v_ref[pl.ds(start, tq), lanes]
            ck = ct_ref[0, h:h + 1, pl.ds(start, tq)]
            s = lax.dot_general(q, kb, _NT, preferred_element_type=f32) + cq - ck
            if diag:
                s = jnp.where(causal, s, NEG)
            m_new = jnp.maximum(m, jnp.max(s, axis=-1, keepdims=True))
            alpha = jnp.exp(m - m_new)
            p = jnp.exp(s - m_new)
            l = alpha * l + jnp.sum(p, axis=-1, keepdims=True)
            acc = alpha * acc + _dot(p.astype(bf16), vb)
            return m_new, l, acc

        init = (jnp.full((tq, 1), -jnp.inf, f32), jnp.zeros((tq, 1), f32), jnp.zeros((tq, hd), f32))
        carry = lax.fori_loop(0, i, functools.partial(block, diag=False), init)
        _, l, acc = block(i, carry, True)
        o_ref[:, lanes] = (acc / l).astype(bf16)


def _fox_attn(qa, kab, vab, c, ct, *, batch, nh, hd, tq):
    n, w_a = qa.shape
    t = n // batch
    nq = t // tq
    return pl.pallas_call(
        functools.partial(_fox_attn_kernel, nh=nh, hd=hd, tq=tq),
        grid=(batch, nq),
        in_specs=[pl.BlockSpec((tq, w_a), lambda b, i: (b * nq + i, 0)),
                  pl.BlockSpec((t, w_a), lambda b, i: (b, 0)),
                  pl.BlockSpec((t, w_a), lambda b, i: (b, 0)),
                  pl.BlockSpec((tq, LANE), lambda b, i: (b * nq + i, 0)),
                  pl.BlockSpec((1, nh, t), lambda b, i: (b, 0, 0))],
        out_specs=pl.BlockSpec((tq, w_a), lambda b, i: (b * nq + i, 0)),
        out_shape=jax.ShapeDtypeStruct((n, w_a), bf16),
        compiler_params=_params(("parallel", "arbitrary"), 40),
        name="fox_attn",
    )(qa, kab, vab, c, ct)


def _gla_kernel(q_ref, k_ref, v_ref, r_ref, fg_ref, wgk_ref, bgk_ref, gout_ref, s0_ref,
                o_ref, st_ref, s_scr, *, nh, dk, dv, chunk, n_chunks, t_valid):
    t = pl.program_id(1)

    @pl.when(t == 0)
    def _():
        s_scr[...] = s0_ref[0]

    tri = _tri(chunk)
    row = lax.broadcasted_iota(jnp.int32, (chunk, chunk), 0)
    col = lax.broadcasted_iota(jnp.int32, (chunk, chunk), 1)
    causal = col <= row

    def do_chunk(ci, carry):
        r0 = pl.multiple_of(ci * chunk, chunk)
        rows = pl.ds(r0, chunk)
        la = _log_sigmoid(_dot(fg_ref[rows, :].astype(bf16), wgk_ref[...]) + bgk_ref[...]) / GK_NORM
        if t_valid is not None:
            la = jnp.where(lax.broadcasted_iota(jnp.int32, la.shape, 0) < t_valid, la, 0.0)
        b = _dot_sel_left(tri, la)
        b_last = b[chunk - 1:chunk, :]
        e_pos = jnp.exp(b)
        e_neg = jnp.exp(-b)
        e_rem = jnp.exp(b_last - b)
        for h in range(nh):
            kl = slice(h * dk, (h + 1) * dk)
            vl = slice(h * dv, (h + 1) * dv)
            q = q_ref[rows, kl].astype(f32)
            k = k_ref[rows, kl].astype(f32)
            v = v_ref[rows, vl]
            qe = (q * e_pos[:, kl]).astype(bf16)
            ke = (k * e_neg[:, kl]).astype(bf16)
            kd = (k * e_rem[:, kl]).astype(bf16)
            s = s_scr[h]
            o_inter = _dot(qe, s.astype(bf16))
            a = lax.dot_general(qe, ke, _NT, preferred_element_type=f32)
            a = jnp.where(causal, a, 0.0)
            o = o_inter + _dot(a.astype(bf16), v)
            decay = jnp.broadcast_to(jnp.exp(b_last[:, kl]), (SUBLANE, dk)).T[:, :1]
            s_scr[h] = s * decay + lax.dot_general(kd, v, _TN, preferred_element_type=f32)
            gated = _rms(o, gout_ref[...]) * jax.nn.silu(r_ref[rows, vl].astype(f32))
            o_ref[rows, vl] = gated.astype(bf16)
        return carry

    lax.fori_loop(0, n_chunks, do_chunk, 0)

    @pl.when(t == pl.num_programs(1) - 1)
    def _():
        st_ref[0] = s_scr[...]


def _gla(qb, kb, vb, rb, fg, wgk_pad, bgk, gout, s0, *, batch, tc, chunk, t_valid):
    n, wk_b = qb.shape
    wv_b = vb.shape[1]
    _, nh, dk, dv = s0.shape
    nt = n // batch // tc
    rowspec = lambda width: pl.BlockSpec((tc, width), lambda b, t: (b * nt + t, 0))
    state = pl.BlockSpec((1, nh, dk, dv), lambda b, t: (b, 0, 0, 0))
    kern = functools.partial(_gla_kernel, nh=nh, dk=dk, dv=dv, chunk=chunk, n_chunks=tc // chunk, t_valid=t_valid)
    return pl.pallas_call(
        kern,
        grid=(batch, nt),
        in_specs=[rowspec(wk_b), rowspec(wk_b), rowspec(wv_b), rowspec(wv_b), rowspec(LANE),
                  _resident((LANE, wk_b)), _resident((1, wk_b)), _resident((1, dv)), state],
        out_specs=[rowspec(wv_b), state],
        out_shape=[jax.ShapeDtypeStruct((n, wv_b), bf16), jax.ShapeDtypeStruct(s0.shape, f32)],
        scratch_shapes=[pltpu.VMEM((nh, dk, dv), f32)],
        compiler_params=_params(("parallel", "arbitrary"), 32),
        name="gla",
    )(qb, kb, vb, rb, fg, wgk_pad, bgk, gout, s0)


def _outproj_kernel(oa_ref, ob_ref, x_ref, w_ref, x1_ref, *, w_a):
    mixed = _dot(oa_ref[...], w_ref[:w_a, :]) + _dot(ob_ref[...], w_ref[w_a:, :])
    x1_ref[...] = x_ref[...] + mixed


def _outproj(oa, ob, x, w_out, *, tm):
    n, d = x.shape
    w_a, wv_b = oa.shape[1], ob.shape[1]
    row = lambda width: pl.BlockSpec((tm, width), lambda i: (i, 0))
    return pl.pallas_call(
        functools.partial(_outproj_kernel, w_a=w_a),
        grid=(n // tm,),
        in_specs=[row(w_a), row(wv_b), row(d), _resident((w_a + wv_b, d))],
        out_specs=row(d),
        out_shape=jax.ShapeDtypeStruct((n, d), f32),
        compiler_params=_params(("parallel",), 40),
        name="outproj",
    )(oa, ob, x, w_out)


def _ffn_kernel(x_ref, g_ref, wup_ref, wdn_ref, o_ref, h_scr):
    @pl.when(pl.program_id(1) == 0)
    def _():
        x = x_ref[...]
        h_scr[...] = _rms(x, g_ref[...]).astype(bf16)
        o_ref[...] = x

    a = _dot(h_scr[...], wup_ref[...])
    a = jnp.square(jnp.maximum(a, 0.0)).astype(bf16)
    o_ref[...] += _dot(a, wdn_ref[...])


def _ffn(x, g, w_up, w_down, *, tm, tf):
    n, d = x.shape
    dff = w_up.shape[1]
    return pl.pallas_call(
        _ffn_kernel,
        grid=(n // tm, dff // tf),
        in_specs=[pl.BlockSpec((tm, d), lambda i, f: (i, 0)), _resident((1, d)),
                  pl.BlockSpec((d, tf), lambda i, f: (0, f)),
                  pl.BlockSpec((tf, d), lambda i, f: (f, 0))],
        out_specs=pl.BlockSpec((tm, d), lambda i, f: (i, 0)),
        out_shape=jax.ShapeDtypeStruct((n, d), f32),
        scratch_shapes=[pltpu.VMEM((tm, d), bf16)],
        compiler_params=_params(("parallel", "arbitrary"), 52),
        name="ffn",
    )(x, g, w_up, w_down)


def _ple_kernel(x_ref, p_ref, wple_ref, gple_ref, ggate_ref, wgate_ref, gfin_ref, y_ref, *, final):
    x = x_ref[...]
    e = _rms(_dot(p_ref[...].astype(bf16), wple_ref[...]), gple_ref[...])
    gate = jax.nn.sigmoid(_dot(_rms(x, ggate_ref[...]).astype(bf16), wgate_ref[...]))
    x = x + gate * e
    y_ref[...] = _rms(x, gfin_ref[...]) if final else x


def _ple(x, p, w_ple, g_ple, g_gate, w_gate, g_final, *, tm, final):
    n, d = x.shape
    pd = p.shape[1]
    row = lambda width: pl.BlockSpec((tm, width), lambda i: (i, 0))
    return pl.pallas_call(
        functools.partial(_ple_kernel, final=final),
        grid=(n // tm,),
        in_specs=[row(d), row(pd), _resident((pd, d)), _resident((1, d)), _resident((1, d)),
                  _resident((d, d)), _resident((1, d))],
        out_specs=row(d),
        out_shape=jax.ShapeDtypeStruct((n, d), f32),
        compiler_params=_params(("parallel",), 40),
        name="ple",
    )(x, p, w_ple, g_ple, g_gate, w_gate, g_final)


def _key_bias_columns(c, nh):
    t = c.shape[0]
    hr = lax.broadcasted_iota(jnp.int32, (nh, LANE), 0)
    hc = lax.broadcasted_iota(jnp.int32, (nh, LANE), 1)
    wide = sum(_dot(term, (hc == hr + nh * j).astype(bf16)) for j, term in enumerate(_split3(c)))
    sub = lax.broadcasted_iota(jnp.int32, (t, nh, LANE), 1)
    ln = lax.broadcasted_iota(jnp.int32, (t, nh, LANE), 2)
    per_row = jnp.where(ln % nh == sub, jnp.broadcast_to(wide[:, None, :], (t, nh, LANE)), 0.0)
    return per_row.reshape(t * nh, LANE).astype(bf16)


def _fox_decode_kernel(pt_ref, q_ref, knew_ref, vnew_ref, lfnew_ref, ck_hbm, cv_hbm, clf_hbm, o_ref,
                       kbuf, vbuf, lbuf, ksem, vsem, lsem, kaug, qaug, cq_scr, m_scr, l_scr, acc_scr,
                       knew_aug, vnew_scr, *, nh, hd, ts, n_pages, ppc):
    step = pl.program_id(0)
    n_steps = pl.num_programs(0)
    nc = n_pages // ppc
    b = step // nc
    c = step % nc
    page = lbuf.shape[2]
    prow = page * nh
    crow = ppc * prow
    rows_q = ts * nh

    def page_copies(hbm, buf, sem, s, slot):
        bb, cc = s // nc, s % nc
        return [pltpu.make_async_copy(hbm.at[pt_ref[bb, cc * ppc + p]], buf.at[slot, pl.ds(p * prow, prow)],
                                      sem.at[slot]) for p in range(ppc)]

    k_copies = functools.partial(page_copies, ck_hbm, kbuf, ksem)
    v_copies = functools.partial(page_copies, cv_hbm, vbuf, vsem)

    def lf_copies(bb, slot):
        return [pltpu.make_async_copy(clf_hbm.at[pt_ref[bb, p]], lbuf.at[slot, p], lsem.at[slot])
                for p in range(n_pages)]

    @pl.when(step == 0)
    def _():
        for cp in k_copies(0, 0) + v_copies(0, 0) + lf_copies(0, 0):
            cp.start()
        knew_aug[...] = jnp.zeros(knew_aug.shape, bf16)
        vnew_scr[...] = jnp.zeros(vnew_scr.shape, bf16)

    @pl.when(step + 1 < n_steps)
    def _():
        nxt = step + 1
        for cp in k_copies(nxt, nxt % 2) + v_copies(nxt, nxt % 2):
            cp.start()

    @pl.when(jnp.logical_and(c == 0, step + nc < n_steps))
    def _():
        for cp in lf_copies(b + 1, (b + 1) % 2):
            cp.start()

    @pl.when(c == 0)
    def _():
        lslot = b % 2
        for cp in lf_copies(b, lslot):
            cp.wait()
        tri = _tri(page)
        carry = jnp.zeros((1, nh), f32)
        for p in range(n_pages):
            cs = _dot_sel_left(tri, lbuf[lslot, p]) + carry
            carry = cs[page - 1:page, :]
            kaug[p * prow:(p + 1) * prow, hd:] = _key_bias_columns(cs, nh)
        lf_new = lfnew_ref[0]
        run, new_rows = carry, []
        for j in range(SUBLANE):
            run = run + lf_new[j:j + 1, :]
            new_rows.append(run)
        c_new = jnp.concatenate(new_rows, axis=0)
        knew_aug[:SUBLANE * nh, hd:] = _key_bias_columns(c_new, nh)
        knew_aug[:rows_q, :hd] = knew_ref[0]
        vnew_scr[:rows_q, :] = vnew_ref[0]
        sub = lax.broadcasted_iota(jnp.int32, (nh, nh), 0)
        ln = lax.broadcasted_iota(jnp.int32, (nh, nh), 1)
        cq_scr[...] = jnp.concatenate(
            [jnp.sum(jnp.where(ln == sub, jnp.broadcast_to(c_new[i:i + 1, :], (nh, nh)), 0.0), axis=-1, keepdims=True)
             for i in range(ts)], axis=0)
        qaug[:, :hd] = q_ref[0]
        lane = lax.broadcasted_iota(jnp.int32, (rows_q, LANE), 1)
        qaug[:, hd:] = jnp.where(lane < 3 * nh, -1.0, 0.0).astype(bf16)
        m_scr[...] = jnp.full(m_scr.shape, -jnp.inf, f32)
        l_scr[...] = jnp.zeros(l_scr.shape, f32)
        acc_scr[...] = jnp.zeros(acc_scr.shape, f32)

    def weights(keys, valid):
        s = lax.dot_general(qaug[...], keys, _NT, preferred_element_type=f32) + cq_scr[...]
        s = jnp.where(valid, s, NEG)
        m = m_scr[...]
        m_new = jnp.maximum(m, jnp.max(s, axis=-1, keepdims=True))
        alpha = jnp.exp(m - m_new)
        p = jnp.exp(s - m_new)
        l_scr[...] = alpha * l_scr[...] + jnp.sum(p, axis=-1, keepdims=True)
        m_scr[...] = m_new
        return p.astype(bf16), alpha

    def accumulate(p, alpha, v):
        acc_scr[...] = alpha * acc_scr[...] + _dot(p, v)

    slot = step % 2
    rows = pl.ds(pl.multiple_of(c * crow, crow), crow)
    for cp in k_copies(step, slot):
        cp.wait()
    kaug[rows, :hd] = kbuf[slot].astype(bf16)
    r = lax.broadcasted_iota(jnp.int32, (rows_q, crow), 0)
    n = lax.broadcasted_iota(jnp.int32, (rows_q, crow), 1)
    p, alpha = weights(kaug[rows, :], n % nh == r % nh)
    for cp in v_copies(step, slot):
        cp.wait()
    accumulate(p, alpha, vbuf[slot].astype(bf16))

    @pl.when(c == nc - 1)
    def _():
        r = lax.broadcasted_iota(jnp.int32, (rows_q, LANE), 0)
        n = lax.broadcasted_iota(jnp.int32, (rows_q, LANE), 1)
        valid = jnp.logical_and(n % nh == r % nh, n // nh <= r // nh)
        p, alpha = weights(knew_aug[...], valid)
        accumulate(p, alpha, vnew_scr[...])
        o_ref[0] = acc_scr[...] / l_scr[...]


def _fox_decode(page_table, q, knew, vnew, lfnew, cache_k, cache_v, cache_lf, *, nh, ppc):
    bs, rows_q, hd = q.shape
    ts = rows_q // nh
    n_pages = page_table.shape[1]
    prow = cache_k.shape[1]
    page = prow // nh
    nc = n_pages // ppc
    assert rows_q <= LANE and ts <= SUBLANE and 3 * nh <= LANE and hd == LANE
    any_spec = pl.BlockSpec(memory_space=pl.ANY)
    per_b = lambda shape: pl.BlockSpec((1,) + shape, lambda s, pt: (s // nc, 0, 0))
    grid_spec = pltpu.PrefetchScalarGridSpec(
        num_scalar_prefetch=1,
        grid=(bs * nc,),
        in_specs=[per_b((rows_q, hd)), per_b((rows_q, hd)), per_b((rows_q, hd)), per_b((SUBLANE, nh)),
                  any_spec, any_spec, any_spec],
        out_specs=per_b((rows_q, hd)),
        scratch_shapes=[
            pltpu.VMEM((2, ppc * prow, hd), f32), pltpu.VMEM((2, ppc * prow, hd), f32),
            pltpu.VMEM((2, n_pages, page, nh), f32),
            pltpu.SemaphoreType.DMA((2,)), pltpu.SemaphoreType.DMA((2,)), pltpu.SemaphoreType.DMA((2,)),
            pltpu.VMEM((n_pages * prow, hd + LANE), bf16), pltpu.VMEM((rows_q, hd + LANE), bf16),
            pltpu.VMEM((rows_q, 1), f32), pltpu.VMEM((rows_q, 1), f32), pltpu.VMEM((rows_q, 1), f32),
            pltpu.VMEM((rows_q, hd), f32),
            pltpu.VMEM((LANE, hd + LANE), bf16), pltpu.VMEM((LANE, hd), bf16),
        ],
    )
    kern = functools.partial(_fox_decode_kernel, nh=nh, hd=hd, ts=ts, n_pages=n_pages, ppc=ppc)
    return pl.pallas_call(
        kern,
        grid_spec=grid_spec,
        out_shape=jax.ShapeDtypeStruct((bs, rows_q, hd), f32),
        compiler_params=_params(("arbitrary",), 48),
        name="fox_decode",
    )(page_table, q, knew, vnew, lfnew, cache_k, cache_v, cache_lf)


def _tile(n, pref):
    while n % pref:
        pref //= 2
    return pref


def kernel(x_prompt, x_sample, cache_k, cache_v, cache_logf, state_gla, page_table, p_prompt, p_sample,
           g_mix, w_in, b_f, w_gk2, b_gk, g_gla_out, w_out, g_mlp, w_up, w_down, w_ple, g_ple, g_ple_gate,
           w_ple_gate, g_final):
    batch, seq, d = x_prompt.shape
    bs, ts, _ = x_sample.shape
    depth = g_mix.shape[0]
    _, n_pool, page, nh, hd = cache_k.shape
    _, _, nh_b, dk, dv = state_gla.shape
    w_a, wk_b, wv_b = nh * hd, nh_b * dk, nh_b * dv
    rank = w_gk2.shape[1]
    n_p, n_s = batch * seq, bs * ts
    assert ts <= SUBLANE and ts % GLA_CHUNK != 0 and seq % GLA_CHUNK == 0
    row = lambda a: a.reshape(1, -1)

    xp = x_prompt.reshape(n_p, d)
    xs = x_sample.reshape(n_s, d)
    outs = {k: [] for k in ("kp", "vp", "fp", "sp", "ks", "vs", "fs", "ss")}
    for l in range(depth):
        final = l == depth - 1
        wl = w_in[l]
        o_f = 3 * w_a
        o_b = o_f + nh
        o_g = o_b + 2 * wk_b + 2 * wv_b
        w_main = jnp.concatenate([wl[:, :o_f], wl[:, o_b:o_g]], axis=1).astype(bf16)
        w_fg = jnp.concatenate([wl[:, o_f:o_b], wl[:, o_g:], jnp.zeros((d, LANE - nh - rank), f32)], axis=1).astype(bf16)
        bf_pad = jnp.zeros((1, LANE), f32).at[0, :nh].set(b_f[l])
        wgk_pad = jnp.zeros((LANE, wk_b), f32).at[nh:nh + rank].set(w_gk2[l]).astype(bf16)
        w_out_b, w_up_b, w_down_b = w_out[l].astype(bf16), w_up[l].astype(bf16), w_down[l].astype(bf16)
        w_ple_b, w_gate_b = w_ple[l].astype(bf16), w_ple_gate[l].astype(bf16)
        inproj = functools.partial(_inproj, g=row(g_mix[l]), w_main=w_main, w_fg=w_fg, nh=nh, hd=hd, wk_b=wk_b,
                                   wv_b=wv_b, scale_a=hd ** -0.5, scale_b=dk ** -0.5)
        gla = functools.partial(_gla, wgk_pad=wgk_pad, bgk=row(b_gk[l]), gout=row(g_gla_out[l]))

        def tail(x, oa, ob, p, n):
            x = _outproj(oa, ob, x, w_out_b, tm=_tile(n, 256))
            x = _ffn(x, row(g_mlp[l]), w_up_b, w_down_b, tm=_tile(n, 512), tf=1024)
            return _ple(x, p.reshape(n, -1), w_ple_b, row(g_ple[l]), row(g_ple_gate[l]), w_gate_b,
                        row(g_final), tm=_tile(n, 256), final=final)

        qa, kab, vab, qb, kb, vb, rb, ka, va, fg = inproj(xp, tm=_tile(n_p, 256))
        logf, c, ct = _fox_prep(fg, bf_pad, batch=batch, nh=nh)
        oa = _fox_attn(qa, kab, vab, c, ct, batch=batch, nh=nh, hd=hd, tq=_tile(seq, 256))
        ob, s_p = gla(qb, kb, vb, rb, fg, s0=jnp.zeros((batch, nh_b, dk, dv), f32), batch=batch,
                      tc=_tile(seq, 256), chunk=GLA_CHUNK, t_valid=None)
        xp = tail(xp, oa, ob, p_prompt[l], n_p)
        outs["kp"].append(ka.reshape(batch, seq, nh, hd))
        outs["vp"].append(va.reshape(batch, seq, nh, hd))
        outs["fp"].append(logf.reshape(batch, seq, nh))
        outs["sp"].append(s_p)

        qa, kab, vab, qb, kb, vb, rb, ka, va, fg = inproj(xs, tm=_tile(n_s, 256))
        logf, _, _ = _fox_prep(fg, bf_pad, batch=1, nh=nh)
        key_rows = lambda a: a.reshape(bs, ts * nh, hd)
        lfnew = jnp.pad(logf.reshape(bs, ts, nh), ((0, 0), (0, SUBLANE - ts), (0, 0)))
        oa = _fox_decode(page_table, key_rows(qa), key_rows(kab), key_rows(vab), lfnew,
                         cache_k[l].reshape(n_pool, page * nh, hd), cache_v[l].reshape(n_pool, page * nh, hd),
                         cache_logf[l], nh=nh, ppc=8)
        oa = oa.reshape(n_s, w_a).astype(bf16)
        tpad = GLA_CHUNK
        flat = lambda a: jnp.pad(a.reshape(bs, ts, -1), ((0, 0), (0, tpad - ts), (0, 0))).reshape(bs * tpad, -1)
        ob, s_s = gla(flat(qb), flat(kb), flat(vb), flat(rb), flat(fg), s0=state_gla[l], batch=bs,
                      tc=tpad, chunk=tpad, t_valid=ts)
        ob = ob.reshape(bs, tpad, wv_b)[:, :ts].reshape(n_s, wv_b)
        xs = tail(xs, oa, ob, p_sample[l], n_s)
        outs["ks"].append(ka.reshape(bs, ts, nh, hd))
        outs["vs"].append(va.reshape(bs, ts, nh, hd))
        outs["fs"].append(logf.reshape(bs, ts, nh))
        outs["ss"].append(s_s)

    st = lambda key: jnp.stack(outs[key])
    return (xp.reshape(batch, seq, d), xs.reshape(bs, ts, d), st("kp"), st("vp"), st("fp"), st("sp"),
            st("ks"), st("vs"), st("fs"), st("ss"))
```

```python
import functools

import jax
import jax.numpy as jnp
from jax import lax
from jax.experimental import pallas as pl
from jax.experimental.pallas import tpu as pltpu

f32 = jnp.float32
bf16 = jnp.bfloat16

EPS = 1e-6
NEG = -1e30
GK_NORM = 16.0
GLA_CHUNK = 64

LANE = 128
SUBLANE = 8
MIB = 2**20

_NT = (((1,), (1,)), ((), ()))
_TN = (((0,), (0,)), ((), ()))


def _params(semantics, vmem_mib):
    return pltpu.CompilerParams(dimension_semantics=semantics, vmem_limit_bytes=vmem_mib * MIB)


def _resident(shape):
    return pl.BlockSpec(shape, lambda *_: (0,) * len(shape), pipeline_mode=pl.Buffered(1))


def _rms(x, g):
    return x * lax.rsqrt(jnp.mean(x * x, axis=-1, keepdims=True) + EPS) * g


def _log_sigmoid(x):
    return jnp.minimum(x, 0.0) - jnp.log1p(jnp.exp(-jnp.abs(x)))


def _split3(x):
    hi = x.astype(bf16)
    r = x - hi.astype(f32)
    mid = r.astype(bf16)
    lo = (r - mid.astype(f32)).astype(bf16)
    return hi, mid, lo


def _dot(a, b):
    return jnp.dot(a, b, preferred_element_type=f32)


def _dot_sel_left(sel, x):
    return sum(_dot(sel, t) for t in _split3(x))


def _tri(n):
    r = lax.broadcasted_iota(jnp.int32, (n, n), 0)
    c = lax.broadcasted_iota(jnp.int32, (n, n), 1)
    return (r >= c).astype(bf16)


def _inproj_kernel(x_ref, g_ref, w_ref, wfg_ref,
                   qa_ref, kab_ref, vab_ref, qb_ref, kb_ref, vb_ref, rb_ref, ka_ref, va_ref, fg_ref,
                   *, w_a, wk_b, wv_b, hd, scale_a, scale_b, chunk):
    h = _rms(x_ref[...], g_ref[...]).astype(bf16)
    col = 0

    def section(width, store):
        nonlocal col
        for j in range(0, width, chunk):
            store(j, _dot(h, w_ref[:, col + j:col + j + chunk]))
        col += width

    def per_head(ref, j, z):
        for t in range(chunk // hd):
            ref[:, j // hd + t, :] = z[:, t * hd:(t + 1) * hd]

    def st_qa(j, z):
        qa_ref[:, j:j + chunk] = (z * scale_a).astype(bf16)

    def st_ka(j, z):
        per_head(ka_ref, j, z)
        kab_ref[:, j:j + chunk] = z.astype(bf16)

    def st_va(j, z):
        per_head(va_ref, j, z)
        vab_ref[:, j:j + chunk] = z.astype(bf16)

    def st_qb(j, z):
        qb_ref[:, j:j + chunk] = (z * scale_b).astype(bf16)

    def st_kb(j, z):
        kb_ref[:, j:j + chunk] = z.astype(bf16)

    def st_vb(j, z):
        vb_ref[:, j:j + chunk] = z.astype(bf16)

    def st_rb(j, z):
        rb_ref[:, j:j + chunk] = z.astype(bf16)

    section(w_a, st_qa)
    section(w_a, st_ka)
    section(w_a, st_va)
    section(wk_b, st_qb)
    section(wk_b, st_kb)
    section(wv_b, st_vb)
    section(wv_b, st_rb)
    fg_ref[...] = _dot(h, wfg_ref[...])


def _inproj(x, g, w_main, w_fg, *, nh, hd, wk_b, wv_b, scale_a, scale_b, tm):
    n, d = x.shape
    cm = w_main.shape[1]
    w_a = nh * hd
    row = lambda width: pl.BlockSpec((tm, width), lambda i: (i, 0))
    heads = pl.BlockSpec((tm, nh, hd), lambda i: (i, 0, 0))
    sds = jax.ShapeDtypeStruct
    kern = functools.partial(_inproj_kernel, w_a=w_a, wk_b=wk_b, wv_b=wv_b, hd=hd,
                             scale_a=scale_a, scale_b=scale_b, chunk=512)
    return pl.pallas_call(
        kern,
        grid=(n // tm,),
        in_specs=[row(d), _resident((1, d)), _resident((d, cm)), _resident((d, LANE))],
        out_specs=[row(w_a), row(w_a), row(w_a), row(wk_b), row(wk_b), row(wv_b), row(wv_b),
                   heads, heads, row(LANE)],
        out_shape=[sds((n, w_a), bf16), sds((n, w_a), bf16), sds((n, w_a), bf16),
                   sds((n, wk_b), bf16), sds((n, wk_b), bf16), sds((n, wv_b), bf16), sds((n, wv_b), bf16),
                   sds((n, nh, hd), f32), sds((n, nh, hd), f32), sds((n, LANE), f32)],
        compiler_params=_params(("parallel",), 52),
        name="inproj",
    )(x, g, w_main, w_fg)


def _fox_prep_kernel(fg_ref, bf_ref, logf_ref, c_ref, ct_ref, *, nh, blk):
    t = fg_ref.shape[0]
    tri = _tri(blk)
    carry = jnp.zeros((1, LANE), f32)
    for b in range(t // blk):
        rows = slice(b * blk, (b + 1) * blk)
        logf = _log_sigmoid(fg_ref[rows, :] + bf_ref[...])
        logf_ref[rows, :] = logf[:, :nh]
        c = _dot_sel_left(tri, logf) + carry
        carry = c[blk - 1:blk, :]
        c_ref[rows, :] = c
        ct_ref[0, :, rows] = c.T[:nh, :]


def _fox_prep(fg, bf_pad, *, batch, nh):
    n = fg.shape[0]
    t = n // batch
    blk = min(t, 256)
    sds = jax.ShapeDtypeStruct
    return pl.pallas_call(
        functools.partial(_fox_prep_kernel, nh=nh, blk=blk),
        grid=(batch,),
        in_specs=[pl.BlockSpec((t, LANE), lambda b: (b, 0)), _resident((1, LANE))],
        out_specs=[pl.BlockSpec((t, nh), lambda b: (b, 0)),
                   pl.BlockSpec((t, LANE), lambda b: (b, 0)),
                   pl.BlockSpec((1, nh, t), lambda b: (b, 0, 0))],
        out_shape=[sds((n, nh), f32), sds((n, LANE), f32), sds((batch, nh, t), f32)],
        compiler_params=_params(("parallel",), 32),
        name="fox_prep",
    )(fg, bf_pad)


def _fox_attn_kernel(q_ref, k_ref, v_ref, c_ref, ct_ref, o_ref, m_scr, l_scr, acc_scr, cq_scr, *, nh, hd, tq):
    i = pl.program_id(1)
    row = lax.broadcasted_iota(jnp.int32, (tq, tq), 0)
    col = lax.broadcasted_iota(jnp.int32, (tq, tq), 1)
    causal = col <= row
    m_scr[...] = jnp.full(m_scr.shape, -jnp.inf, f32)
    l_scr[...] = jnp.zeros(l_scr.shape, f32)
    acc_scr[...] = jnp.zeros(acc_scr.shape, f32)
    for h in range(nh):
        cq_scr[h] = jnp.broadcast_to(c_ref[:, h:h + 1], (tq, LANE))

    def weights(h, t):
        m, cq = m_scr[h], cq_scr[h]
        m_new = jnp.maximum(m, jnp.max(t, axis=-1, keepdims=True) + cq)
        alpha = jnp.exp(m - m_new)
        p = jnp.exp(t + jnp.concatenate([cq - m_new] * (tq // LANE), axis=1))
        l_scr[h] = alpha * l_scr[h] + jnp.sum(p, axis=-1, keepdims=True)
        m_scr[h] = m_new
        return p.astype(bf16), alpha

    def block(j, diag):
        keys = pl.ds(pl.multiple_of(j * tq, tq), tq)
        for h0 in range(0, nh, 2):
            ps, alphas = [], []
            for h in (h0, h0 + 1):
                lanes = slice(h * hd, (h + 1) * hd)
                t = lax.dot_general(q_ref[:, lanes], k_ref[keys, lanes], _NT, preferred_element_type=f32)
                t = t - ct_ref[0, h:h + 1, keys]
                if diag:
                    t = jnp.where(causal, t, NEG)
                p, alpha = weights(h, t)
                ps.append(p)
                alphas.append(alpha)
            pv = _dot(jnp.concatenate(ps, axis=0), v_ref[keys, h0 * hd:(h0 + 2) * hd])
            acc_scr[h0] = alphas[0] * acc_scr[h0] + pv[:tq, :hd]
            acc_scr[h0 + 1] = alphas[1] * acc_scr[h0 + 1] + pv[tq:, hd:]

    def off_diagonal(j, carry):
        block(j, False)
        return carry

    lax.fori_loop(0, i, off_diagonal, 0)
    block(i, True)
    for h in range(nh):
        o_ref[:, h * hd:(h + 1) * hd] = (acc_scr[h] / l_scr[h]).astype(bf16)


def _fox_attn(qa, kab, vab, c, ct, *, batch, nh, hd, tq):
    n, w_a = qa.shape
    t = n // batch
    nq = t // tq
    assert hd == LANE and nh % 2 == 0 and tq % LANE == 0
    return pl.pallas_call(
        functools.partial(_fox_attn_kernel, nh=nh, hd=hd, tq=tq),
        grid=(batch, nq),
        in_specs=[pl.BlockSpec((tq, w_a), lambda b, i: (b * nq + i, 0)),
                  pl.BlockSpec((t, w_a), lambda b, i: (b, 0)),
                  pl.BlockSpec((t, w_a), lambda b, i: (b, 0)),
                  pl.BlockSpec((tq, LANE), lambda b, i: (b * nq + i, 0)),
                  pl.BlockSpec((1, nh, t), lambda b, i: (b, 0, 0))],
        out_specs=pl.BlockSpec((tq, w_a), lambda b, i: (b * nq + i, 0)),
        out_shape=jax.ShapeDtypeStruct((n, w_a), bf16),
        scratch_shapes=[pltpu.VMEM((nh, tq, LANE), f32)] * 4,
        compiler_params=_params(("parallel", "arbitrary"), 40),
        name="fox_attn",
    )(qa, kab, vab, c, ct)


def _gla_kernel(q_ref, k_ref, v_ref, r_ref, fg_ref, wgk_ref, bgk_ref, gout_ref, s0_ref,
                o_ref, st_ref, s_scr, *, nh, dk, dv, chunk, n_chunks, n_seq, t_valid):
    t = pl.program_id(1)

    @pl.when(t == 0)
    def _():
        s_scr[...] = s0_ref[...]

    tri = _tri(chunk)
    row = lax.broadcasted_iota(jnp.int32, (chunk, chunk), 0)
    col = lax.broadcasted_iota(jnp.int32, (chunk, chunk), 1)
    causal = col <= row

    def do_chunk(e, ci):
        rows = slice((e * n_chunks + ci) * chunk, (e * n_chunks + ci + 1) * chunk)
        la = _log_sigmoid(_dot(fg_ref[rows, :].astype(bf16), wgk_ref[...]) + bgk_ref[...]) / GK_NORM
        if t_valid is not None:
            la = jnp.where(lax.broadcasted_iota(jnp.int32, la.shape, 0) < t_valid, la, 0.0)
        b = _dot_sel_left(tri, la)
        b_last = b[chunk - 1:chunk, :]
        e_pos = jnp.exp(b)
        e_neg = jnp.exp(-b)
        e_rem = jnp.exp(b_last - b)
        for h in range(nh):
            kl = slice(h * dk, (h + 1) * dk)
            vl = slice(h * dv, (h + 1) * dv)
            q = q_ref[rows, kl].astype(f32)
            k = k_ref[rows, kl].astype(f32)
            v = v_ref[rows, vl]
            qe = (q * e_pos[:, kl]).astype(bf16)
            ke = (k * e_neg[:, kl]).astype(bf16)
            kd = (k * e_rem[:, kl]).astype(bf16)
            s = s_scr[e, h]
            o_inter = _dot(qe, s.astype(bf16))
            a = lax.dot_general(qe, ke, _NT, preferred_element_type=f32)
            a = jnp.where(causal, a, 0.0)
            o = o_inter + _dot(a.astype(bf16), v)
            decay = jnp.broadcast_to(jnp.exp(b_last[:, kl]), (SUBLANE, dk)).T[:, :1]
            s_scr[e, h] = s * decay + lax.dot_general(kd, v, _TN, preferred_element_type=f32)
            gated = _rms(o, gout_ref[...]) * jax.nn.silu(r_ref[rows, vl].astype(f32))
            o_ref[rows, vl] = gated.astype(bf16)

    for e in range(n_seq):
        for ci in range(n_chunks):
            do_chunk(e, ci)

    @pl.when(t == pl.num_programs(1) - 1)
    def _():
        st_ref[...] = s_scr[...]


def _gla(qb, kb, vb, rb, fg, wgk_pad, bgk, gout, s0, *, batch, tc, n_seq, chunk, t_valid):
    n, wk_b = qb.shape
    wv_b = vb.shape[1]
    _, nh, dk, dv = s0.shape
    nt = n // batch // tc
    assert n_seq == 1 or nt == 1
    rowspec = lambda width: pl.BlockSpec((n_seq * tc, width), lambda b, t: (b * nt + t, 0))
    state = pl.BlockSpec((n_seq, nh, dk, dv), lambda b, t: (b, 0, 0, 0))
    kern = functools.partial(_gla_kernel, nh=nh, dk=dk, dv=dv, chunk=chunk, n_chunks=tc // chunk, n_seq=n_seq,
                             t_valid=t_valid)
    return pl.pallas_call(
        kern,
        grid=(batch // n_seq, nt),
        in_specs=[rowspec(wk_b), rowspec(wk_b), rowspec(wv_b), rowspec(wv_b), rowspec(LANE),
                  _resident((LANE, wk_b)), _resident((1, wk_b)), _resident((1, dv)), state],
        out_specs=[rowspec(wv_b), state],
        out_shape=[jax.ShapeDtypeStruct((n, wv_b), bf16), jax.ShapeDtypeStruct(s0.shape, f32)],
        scratch_shapes=[pltpu.VMEM((n_seq, nh, dk, dv), f32)],
        compiler_params=_params(("parallel", "arbitrary"), 32),
        name="gla",
    )(qb, kb, vb, rb, fg, wgk_pad, bgk, gout, s0)


def _outproj_kernel(oa_ref, ob_ref, x_ref, w_ref, x1_ref, *, w_a):
    mixed = _dot(oa_ref[...], w_ref[:w_a, :]) + _dot(ob_ref[...], w_ref[w_a:, :])
    x1_ref[...] = x_ref[...] + mixed


def _outproj(oa, ob, x, w_out, *, tm):
    n, d = x.shape
    w_a, wv_b = oa.shape[1], ob.shape[1]
    row = lambda width: pl.BlockSpec((tm, width), lambda i: (i, 0))
    return pl.pallas_call(
        functools.partial(_outproj_kernel, w_a=w_a),
        grid=(n // tm,),
        in_specs=[row(w_a), row(wv_b), row(d), _resident((w_a + wv_b, d))],
        out_specs=row(d),
        out_shape=jax.ShapeDtypeStruct((n, d), f32),
        compiler_params=_params(("parallel",), 40),
        name="outproj",
    )(oa, ob, x, w_out)


def _ffn_kernel(x_ref, g_ref, wup_ref, wdn_ref, o_ref, h_scr):
    @pl.when(pl.program_id(1) == 0)
    def _():
        x = x_ref[...]
        h_scr[...] = _rms(x, g_ref[...]).astype(bf16)
        o_ref[...] = x

    a = _dot(h_scr[...], wup_ref[...])
    a = jnp.square(jnp.maximum(a, 0.0)).astype(bf16)
    o_ref[...] += _dot(a, wdn_ref[...])


def _ffn(x, g, w_up, w_down, *, tm, tf):
    n, d = x.shape
    dff = w_up.shape[1]
    return pl.pallas_call(
        _ffn_kernel,
        grid=(n // tm, dff // tf),
        in_specs=[pl.BlockSpec((tm, d), lambda i, f: (i, 0)), _resident((1, d)),
                  pl.BlockSpec((d, tf), lambda i, f: (0, f)),
                  pl.BlockSpec((tf, d), lambda i, f: (f, 0))],
        out_specs=pl.BlockSpec((tm, d), lambda i, f: (i, 0)),
        out_shape=jax.ShapeDtypeStruct((n, d), f32),
        scratch_shapes=[pltpu.VMEM((tm, d), bf16)],
        compiler_params=_params(("parallel", "arbitrary"), 52),
        name="ffn",
    )(x, g, w_up, w_down)


def _ple_kernel(x_ref, p_ref, wple_ref, gple_ref, ggate_ref, wgate_ref, gfin_ref, y_ref, *, final):
    x = x_ref[...]
    e = _rms(_dot(p_ref[...].astype(bf16), wple_ref[...]), gple_ref[...])
    gate = jax.nn.sigmoid(_dot(_rms(x, ggate_ref[...]).astype(bf16), wgate_ref[...]))
    x = x + gate * e
    y_ref[...] = _rms(x, gfin_ref[...]) if final else x


def _ple(x, p, w_ple, g_ple, g_gate, w_gate, g_final, *, tm, final):
    n, d = x.shape
    pd = p.shape[1]
    row = lambda width: pl.BlockSpec((tm, width), lambda i: (i, 0))
    return pl.pallas_call(
        functools.partial(_ple_kernel, final=final),
        grid=(n // tm,),
        in_specs=[row(d), row(pd), _resident((pd, d)), _resident((1, d)), _resident((1, d)),
                  _resident((d, d)), _resident((1, d))],
        out_specs=row(d),
        out_shape=jax.ShapeDtypeStruct((n, d), f32),
        compiler_params=_params(("parallel",), 40),
        name="ple",
    )(x, p, w_ple, g_ple, g_gate, w_gate, g_final)


def _scan_tokens(x, nh, n_tokens):
    lane = lax.broadcasted_iota(jnp.int32, x.shape, 1)
    shift = nh
    while shift < n_tokens * nh:
        x = x + jnp.where(lane >= shift, pltpu.roll(x, shift, axis=1), 0.0)
        shift *= 2
    return x


def _spread_last_group(x, nh):
    shift = nh
    while shift < LANE:
        x = x + pltpu.roll(x, shift, axis=1)
        shift *= 2
    return x


def _fox_decode_kernel(pt_ref, q_ref, knew_ref, vnew_ref, lfnew_ref, ck_hbm, cv_hbm, clf_hbm, o_ref,
                       kbuf, vbuf, lbuf, ksem, vsem, lsem, k2, v2, knew_scr, vnew_scr,
                       *, nh, hd, ts, n_pages, ppc):
    b = pl.program_id(0)
    prow = lbuf.shape[2]
    page = prow // nh
    hp = ppc // 2
    half = hp * prow
    rows_q = ts * nh

    def page_copies(hbm, buf, sem, rows, bb, slot):
        return [pltpu.make_async_copy(hbm.at[pt_ref[bb, p]], buf.at[slot, pl.ds(p * rows, rows)], sem.at[slot])
                for p in range(n_pages)]

    def all_copies(bb, slot):
        return (page_copies(ck_hbm, kbuf, ksem, prow, bb, slot) + page_copies(cv_hbm, vbuf, vsem, prow, bb, slot)
                + page_copies(clf_hbm, lbuf, lsem, 1, bb, slot))

    @pl.when(b == 0)
    def _():
        for cp in all_copies(0, 0):
            cp.start()
        knew_scr[...] = jnp.zeros(knew_scr.shape, bf16)
        vnew_scr[...] = jnp.zeros(vnew_scr.shape, bf16)

    @pl.when(b + 1 < pl.num_programs(0))
    def _():
        for cp in all_copies(b + 1, (b + 1) % 2):
            cp.start()

    slot = b % 2
    for cp in all_copies(b, slot):
        cp.wait()

    within = _scan_tokens(lbuf[slot], nh, page)
    lane = lax.broadcasted_iota(jnp.int32, (n_pages, LANE), 1)
    totals = jnp.where(lane >= LANE - nh, within[:, prow - LANE:], 0.0)
    pr = lax.broadcasted_iota(jnp.int32, (n_pages, n_pages), 0)
    pc = lax.broadcasted_iota(jnp.int32, (n_pages, n_pages), 1)
    before = _dot_sel_left((pr > pc).astype(bf16), totals)
    cflat = within + jnp.concatenate([_spread_last_group(before, nh)] * (prow // LANE), axis=1)
    whole = _spread_last_group(before[n_pages - 1:, :] + totals[n_pages - 1:, :], nh)
    c_new = _scan_tokens(lfnew_ref[0], nh, LANE // nh) + whole
    rr = lax.broadcasted_iota(jnp.int32, (rows_q, LANE), 0)
    ll = lax.broadcasted_iota(jnp.int32, (rows_q, LANE), 1)
    cq = jnp.sum(jnp.where(ll == rr, jnp.broadcast_to(c_new, (rows_q, LANE)), 0.0),
                 axis=-1, keepdims=True)
    q = q_ref[0]
    zero = jnp.zeros(q.shape, bf16)
    q2 = jnp.concatenate([jnp.concatenate([q, zero], axis=1), jnp.concatenate([zero, q], axis=1)], axis=0)

    def both(x):
        return jnp.concatenate([x, x], axis=0)

    m = jnp.full((rows_q, 1), -jnp.inf, f32)
    l = jnp.zeros((rows_q, 1), f32)
    acc = jnp.zeros((2 * rows_q, 2 * hd), f32)
    for blk in range(n_pages // ppc):
        row_a = blk * ppc * prow
        k2[blk, :, :hd] = kbuf[slot, row_a:row_a + half].astype(bf16)
        k2[blk, :, hd:] = kbuf[slot, row_a + half:row_a + 2 * half].astype(bf16)
        s = lax.dot_general(q2, k2[blk], _NT, preferred_element_type=f32)
        pg = blk * ppc
        bias = jnp.concatenate(
            [jnp.concatenate([jnp.broadcast_to(cflat[pg + p:pg + p + 1, :], (rows_q, prow)),
                              jnp.broadcast_to(cflat[pg + hp + p:pg + hp + p + 1, :], (rows_q, prow))], axis=0)
             for p in range(hp)], axis=1)
        r = lax.broadcasted_iota(jnp.int32, s.shape, 0)
        n = lax.broadcasted_iota(jnp.int32, s.shape, 1)
        t = jnp.where(n % nh == r % nh, s - bias, NEG)
        rowmax = jnp.max(t, axis=-1, keepdims=True)
        m_new = jnp.maximum(m, jnp.maximum(rowmax[:rows_q], rowmax[rows_q:]) + cq)
        alpha = jnp.exp(m - m_new)
        p = jnp.exp(t + both(cq - m_new))
        rowsum = jnp.sum(p, axis=-1, keepdims=True)
        l = alpha * l + rowsum[:rows_q] + rowsum[rows_q:]
        m = m_new
        v2[blk, :, :hd] = vbuf[slot, row_a:row_a + half].astype(bf16)
        v2[blk, :, hd:] = vbuf[slot, row_a + half:row_a + 2 * half].astype(bf16)
        acc = both(alpha) * acc + _dot(p.astype(bf16), v2[blk])

    knew_scr[:rows_q, :] = knew_ref[0]
    vnew_scr[:rows_q, :] = vnew_ref[0]
    s = lax.dot_general(q, knew_scr[...], _NT, preferred_element_type=f32)
    r = lax.broadcasted_iota(jnp.int32, s.shape, 0)
    n = lax.broadcasted_iota(jnp.int32, s.shape, 1)
    valid = jnp.logical_and(n % nh == r % nh, n // nh <= r // nh)
    t = jnp.where(valid, s - c_new, NEG)
    m_new = jnp.maximum(m, jnp.max(t, axis=-1, keepdims=True) + cq)
    alpha = jnp.exp(m - m_new)
    p = jnp.exp(t + (cq - m_new))
    l = alpha * l + jnp.sum(p, axis=-1, keepdims=True)
    acc = both(alpha) * acc
    o = acc[:rows_q, :hd] + acc[rows_q:, hd:] + _dot(p.astype(bf16), vnew_scr[...])
    o_ref[0] = o / l


def _fox_decode(page_table, q, knew, vnew, lfnew, cache_k, cache_v, cache_lf, *, nh, ppc):
    bs, rows_q, hd = q.shape
    ts = rows_q // nh
    n_pages = page_table.shape[1]
    prow = cache_k.shape[1]
    assert rows_q <= LANE and hd == LANE and prow % LANE == 0 and LANE % nh == 0
    assert ppc % 2 == 0 and n_pages % ppc == 0
    any_spec = pl.BlockSpec(memory_space=pl.ANY)
    per_b = lambda shape: pl.BlockSpec((1,) + shape, lambda s, pt: (s, 0, 0))
    half = ppc // 2 * prow
    grid_spec = pltpu.PrefetchScalarGridSpec(
        num_scalar_prefetch=1,
        grid=(bs,),
        in_specs=[per_b((rows_q, hd)), per_b((rows_q, hd)), per_b((rows_q, hd)), per_b((1, LANE)),
                  any_spec, any_spec, any_spec],
        out_specs=per_b((rows_q, hd)),
        scratch_shapes=[
            pltpu.VMEM((2, n_pages * prow, hd), f32), pltpu.VMEM((2, n_pages * prow, hd), f32),
            pltpu.VMEM((2, n_pages, prow), f32),
            pltpu.SemaphoreType.DMA((2,)), pltpu.SemaphoreType.DMA((2,)), pltpu.SemaphoreType.DMA((2,)),
            pltpu.VMEM((n_pages // ppc, half, 2 * hd), bf16), pltpu.VMEM((n_pages // ppc, half, 2 * hd), bf16),
            pltpu.VMEM((LANE, hd), bf16), pltpu.VMEM((LANE, hd), bf16),
        ],
    )
    kern = functools.partial(_fox_decode_kernel, nh=nh, hd=hd, ts=ts, n_pages=n_pages, ppc=ppc)
    return pl.pallas_call(
        kern,
        grid_spec=grid_spec,
        out_shape=jax.ShapeDtypeStruct((bs, rows_q, hd), f32),
        compiler_params=_params(("arbitrary",), 56),
        name="fox_decode",
    )(page_table, q, knew, vnew, lfnew, cache_k, cache_v, cache_lf)


def _tile(n, pref):
    while n % pref:
        pref //= 2
    return pref


def kernel(x_prompt, x_sample, cache_k, cache_v, cache_logf, state_gla, page_table, p_prompt, p_sample,
           g_mix, w_in, b_f, w_gk2, b_gk, g_gla_out, w_out, g_mlp, w_up, w_down, w_ple, g_ple, g_ple_gate,
           w_ple_gate, g_final):
    batch, seq, d = x_prompt.shape
    bs, ts, _ = x_sample.shape
    depth = g_mix.shape[0]
    _, n_pool, page, nh, hd = cache_k.shape
    _, _, nh_b, dk, dv = state_gla.shape
    w_a, wk_b, wv_b = nh * hd, nh_b * dk, nh_b * dv
    rank = w_gk2.shape[1]
    n_p, n_s = batch * seq, bs * ts
    assert ts <= SUBLANE and ts % GLA_CHUNK != 0 and seq % GLA_CHUNK == 0
    row = lambda a: a.reshape(1, -1)

    xp = x_prompt.reshape(n_p, d)
    xs = x_sample.reshape(n_s, d)
    outs = {k: [] for k in ("kp", "vp", "fp", "sp", "ks", "vs", "fs", "ss")}
    for l in range(depth):
        final = l == depth - 1
        wl = w_in[l]
        o_f = 3 * w_a
        o_b = o_f + nh
        o_g = o_b + 2 * wk_b + 2 * wv_b
        w_main = jnp.concatenate([wl[:, :o_f], wl[:, o_b:o_g]], axis=1).astype(bf16)
        w_fg = jnp.concatenate([wl[:, o_f:o_b], wl[:, o_g:], jnp.zeros((d, LANE - nh - rank), f32)], axis=1).astype(bf16)
        bf_pad = jnp.zeros((1, LANE), f32).at[0, :nh].set(b_f[l])
        wgk_pad = jnp.zeros((LANE, wk_b), f32).at[nh:nh + rank].set(w_gk2[l]).astype(bf16)
        w_out_b, w_up_b, w_down_b = w_out[l].astype(bf16), w_up[l].astype(bf16), w_down[l].astype(bf16)
        w_ple_b, w_gate_b = w_ple[l].astype(bf16), w_ple_gate[l].astype(bf16)
        inproj = functools.partial(_inproj, g=row(g_mix[l]), w_main=w_main, w_fg=w_fg, nh=nh, hd=hd, wk_b=wk_b,
                                   wv_b=wv_b, scale_a=hd ** -0.5, scale_b=dk ** -0.5)
        gla = functools.partial(_gla, wgk_pad=wgk_pad, bgk=row(b_gk[l]), gout=row(g_gla_out[l]))

        def tail(x, oa, ob, p, n):
            x = _outproj(oa, ob, x, w_out_b, tm=_tile(n, 256))
            x = _ffn(x, row(g_mlp[l]), w_up_b, w_down_b, tm=_tile(n, 512), tf=1024)
            return _ple(x, p.reshape(n, -1), w_ple_b, row(g_ple[l]), row(g_ple_gate[l]), w_gate_b,
                        row(g_final), tm=_tile(n, 256), final=final)

        qa, kab, vab, qb, kb, vb, rb, ka, va, fg = inproj(xp, tm=_tile(n_p, 256))
        logf, c, ct = _fox_prep(fg, bf_pad, batch=batch, nh=nh)
        oa = _fox_attn(qa, kab, vab, c, ct, batch=batch, nh=nh, hd=hd, tq=_tile(seq, 256))
        ob, s_p = gla(qb, kb, vb, rb, fg, s0=jnp.zeros((batch, nh_b, dk, dv), f32), batch=batch,
                      tc=_tile(seq, 256), n_seq=1, chunk=GLA_CHUNK, t_valid=None)
        xp = tail(xp, oa, ob, p_prompt[l], n_p)
        outs["kp"].append(ka.reshape(batch, seq, nh, hd))
        outs["vp"].append(va.reshape(batch, seq, nh, hd))
        outs["fp"].append(logf.reshape(batch, seq, nh))
        outs["sp"].append(s_p)

        qa, kab, vab, qb, kb, vb, rb, ka, va, fg = inproj(xs, tm=_tile(n_s, 256))
        logf, _, _ = _fox_prep(fg, bf_pad, batch=1, nh=nh)
        key_rows = lambda a: a.reshape(bs, ts * nh, hd)
        lfnew = jnp.pad(logf.reshape(bs, 1, ts * nh), ((0, 0), (0, 0), (0, LANE - ts * nh)))
        oa = _fox_decode(page_table, key_rows(qa), key_rows(kab), key_rows(vab), lfnew,
                         cache_k[l].reshape(n_pool, page * nh, hd), cache_v[l].reshape(n_pool, page * nh, hd),
                         cache_logf[l].reshape(n_pool, 1, page * nh), nh=nh, ppc=4)
        oa = oa.reshape(n_s, w_a).astype(bf16)
        tpad = GLA_CHUNK
        flat = lambda a: jnp.pad(a.reshape(bs, ts, -1), ((0, 0), (0, tpad - ts), (0, 0))).reshape(bs * tpad, -1)
        ob, s_s = gla(flat(qb), flat(kb), flat(vb), flat(rb), flat(fg), s0=state_gla[l], batch=bs,
                      tc=tpad, n_seq=_tile(bs, 4), chunk=tpad, t_valid=ts)
        ob = ob.reshape(bs, tpad, wv_b)[:, :ts].reshape(n_s, wv_b)
        xs = tail(xs, oa, ob, p_sample[l], n_s)
        outs["ks"].append(ka.reshape(bs, ts, nh, hd))
        outs["vs"].append(va.reshape(bs, ts, nh, hd))
        outs["fs"].append(logf.reshape(bs, ts, nh))
        outs["ss"].append(s_s)

    st = lambda key: jnp.stack(outs[key])
    return (xp.reshape(batch, seq, d), xs.reshape(bs, ts, d), st("kp"), st("vp"), st("fp"), st("sp"),
            st("ks"), st("vs"), st("fs"), st("ss"))
```

```python
import functools

import jax
import jax.numpy as jnp
from jax import lax
from jax.experimental import pallas as pl
from jax.experimental.pallas import tpu as pltpu

f32 = jnp.float32
bf16 = jnp.bfloat16

EPS = 1e-6
NEG = -1e30
GK_NORM = 16.0
GLA_CHUNK = 64

LANE = 128
SUBLANE = 8
MIB = 2**20

_NT = (((1,), (1,)), ((), ()))
_TN = (((0,), (0,)), ((), ()))


def _params(semantics, vmem_mib):
    return pltpu.CompilerParams(dimension_semantics=semantics, vmem_limit_bytes=vmem_mib * MIB)


def _resident(shape):
    return pl.BlockSpec(shape, lambda *_: (0,) * len(shape), pipeline_mode=pl.Buffered(1))


def _rms(x, g):
    return x * lax.rsqrt(jnp.mean(x * x, axis=-1, keepdims=True) + EPS) * g


def _log_sigmoid(x):
    return jnp.minimum(x, 0.0) - jnp.log1p(jnp.exp(-jnp.abs(x)))


def _split3(x):
    hi = x.astype(bf16)
    r = x - hi.astype(f32)
    mid = r.astype(bf16)
    lo = (r - mid.astype(f32)).astype(bf16)
    return hi, mid, lo


def _dot(a, b):
    return jnp.dot(a, b, preferred_element_type=f32)


def _dot_sel_left(sel, x):
    return sum(_dot(sel, t) for t in _split3(x))


def _tri(n):
    r = lax.broadcasted_iota(jnp.int32, (n, n), 0)
    c = lax.broadcasted_iota(jnp.int32, (n, n), 1)
    return (r >= c).astype(bf16)


def _inproj_kernel(x_ref, g_ref, w_ref, wfg_ref,
                   qa_ref, kab_ref, vab_ref, qb_ref, kb_ref, vb_ref, rb_ref, ka_ref, va_ref, fg_ref,
                   *, w_a, wk_b, wv_b, scale_a, scale_b, chunk):
    h = _rms(x_ref[...], g_ref[...]).astype(bf16)
    col = 0

    def section(width, store):
        nonlocal col
        for j in range(0, width, chunk):
            store(j, _dot(h, w_ref[:, col + j:col + j + chunk]))
        col += width

    def st_qa(j, z):
        qa_ref[:, j:j + chunk] = (z * scale_a).astype(bf16)

    def st_ka(j, z):
        ka_ref[:, j:j + chunk] = z
        kab_ref[:, j:j + chunk] = z.astype(bf16)

    def st_va(j, z):
        va_ref[:, j:j + chunk] = z
        vab_ref[:, j:j + chunk] = z.astype(bf16)

    def st_qb(j, z):
        qb_ref[:, j:j + chunk] = (z * scale_b).astype(bf16)

    def st_kb(j, z):
        kb_ref[:, j:j + chunk] = z.astype(bf16)

    def st_vb(j, z):
        vb_ref[:, j:j + chunk] = z.astype(bf16)

    def st_rb(j, z):
        rb_ref[:, j:j + chunk] = z.astype(bf16)

    section(w_a, st_qa)
    section(w_a, st_ka)
    section(w_a, st_va)
    section(wk_b, st_qb)
    section(wk_b, st_kb)
    section(wv_b, st_vb)
    section(wv_b, st_rb)
    fg_ref[...] = _dot(h, wfg_ref[...])


def _inproj(x, g, w_main, w_fg, *, nh, hd, wk_b, wv_b, scale_a, scale_b, tm):
    n, d = x.shape
    cm = w_main.shape[1]
    w_a = nh * hd
    row = lambda width: pl.BlockSpec((tm, width), lambda i: (i, 0))
    sds = jax.ShapeDtypeStruct
    kern = functools.partial(_inproj_kernel, w_a=w_a, wk_b=wk_b, wv_b=wv_b,
                             scale_a=scale_a, scale_b=scale_b, chunk=512)
    return pl.pallas_call(
        kern,
        grid=(n // tm,),
        in_specs=[row(d), _resident((1, d)), _resident((d, cm)), _resident((d, LANE))],
        out_specs=[row(w_a), row(w_a), row(w_a), row(wk_b), row(wk_b), row(wv_b), row(wv_b),
                   row(w_a), row(w_a), row(LANE)],
        out_shape=[sds((n, w_a), bf16), sds((n, w_a), bf16), sds((n, w_a), bf16),
                   sds((n, wk_b), bf16), sds((n, wk_b), bf16), sds((n, wv_b), bf16), sds((n, wv_b), bf16),
                   sds((n, w_a), f32), sds((n, w_a), f32), sds((n, LANE), f32)],
        compiler_params=_params(("parallel",), 52),
        name="inproj",
    )(x, g, w_main, w_fg)


def _fox_prep_kernel(fg_ref, bf_ref, logf_ref, c_ref, ct_ref, *, nh, blk):
    t = fg_ref.shape[0]
    tri = _tri(blk)
    carry = jnp.zeros((1, LANE), f32)
    for b in range(t // blk):
        rows = slice(b * blk, (b + 1) * blk)
        logf = _log_sigmoid(fg_ref[rows, :] + bf_ref[...])
        logf_ref[rows, :] = logf[:, :nh]
        c = _dot_sel_left(tri, logf) + carry
        carry = c[blk - 1:blk, :]
        c_ref[rows, :] = c
        ct_ref[0, :, rows] = c.T[:nh, :]


def _fox_prep(fg, bf_pad, *, batch, nh):
    n = fg.shape[0]
    t = n // batch
    blk = min(t, 256)
    sds = jax.ShapeDtypeStruct
    return pl.pallas_call(
        functools.partial(_fox_prep_kernel, nh=nh, blk=blk),
        grid=(batch,),
        in_specs=[pl.BlockSpec((t, LANE), lambda b: (b, 0)), _resident((1, LANE))],
        out_specs=[pl.BlockSpec((t, nh), lambda b: (b, 0)),
                   pl.BlockSpec((t, LANE), lambda b: (b, 0)),
                   pl.BlockSpec((1, nh, t), lambda b: (b, 0, 0))],
        out_shape=[sds((n, nh), f32), sds((n, LANE), f32), sds((batch, nh, t), f32)],
        compiler_params=_params(("parallel",), 32),
        name="fox_prep",
    )(fg, bf_pad)


def _fox_attn_kernel(q_ref, k_ref, v_ref, c_ref, ct_ref, o_ref, m_scr, l_scr, acc_scr, cq_scr, *, nh, hd, tq):
    i = pl.program_id(1)
    row = lax.broadcasted_iota(jnp.int32, (tq, tq), 0)
    col = lax.broadcasted_iota(jnp.int32, (tq, tq), 1)
    causal = col <= row
    m_scr[...] = jnp.full(m_scr.shape, -jnp.inf, f32)
    l_scr[...] = jnp.zeros(l_scr.shape, f32)
    acc_scr[...] = jnp.zeros(acc_scr.shape, f32)
    for h in range(nh):
        cq_scr[h] = jnp.broadcast_to(c_ref[:, h:h + 1], (tq, LANE))

    def weights(h, t):
        m, cq = m_scr[h], cq_scr[h]
        m_new = jnp.maximum(m, jnp.max(t, axis=-1, keepdims=True) + cq)
        alpha = jnp.exp(m - m_new)
        p = jnp.exp(t + jnp.concatenate([cq - m_new] * (tq // LANE), axis=1))
        l_scr[h] = alpha * l_scr[h] + jnp.sum(p, axis=-1, keepdims=True)
        m_scr[h] = m_new
        return p.astype(bf16), alpha

    def block(j, diag):
        keys = pl.ds(pl.multiple_of(j * tq, tq), tq)
        for h0 in range(0, nh, 2):
            ps, alphas = [], []
            for h in (h0, h0 + 1):
                lanes = slice(h * hd, (h + 1) * hd)
                t = lax.dot_general(q_ref[:, lanes], k_ref[keys, lanes], _NT, preferred_element_type=f32)
                t = t - ct_ref[0, h:h + 1, keys]
                if diag:
                    t = jnp.where(causal, t, NEG)
                p, alpha = weights(h, t)
                ps.append(p)
                alphas.append(alpha)
            pv = _dot(jnp.concatenate(ps, axis=0), v_ref[keys, h0 * hd:(h0 + 2) * hd])
            acc_scr[h0] = alphas[0] * acc_scr[h0] + pv[:tq, :hd]
            acc_scr[h0 + 1] = alphas[1] * acc_scr[h0 + 1] + pv[tq:, hd:]

    def off_diagonal(j, carry):
        block(j, False)
        return carry

    lax.fori_loop(0, i, off_diagonal, 0)
    block(i, True)
    for h in range(nh):
        o_ref[:, h * hd:(h + 1) * hd] = (acc_scr[h] / l_scr[h]).astype(bf16)


def _fox_attn(qa, kab, vab, c, ct, *, batch, nh, hd, tq):
    n, w_a = qa.shape
    t = n // batch
    nq = t // tq
    assert hd == LANE and nh % 2 == 0 and tq % LANE == 0
    return pl.pallas_call(
        functools.partial(_fox_attn_kernel, nh=nh, hd=hd, tq=tq),
        grid=(batch, nq),
        in_specs=[pl.BlockSpec((tq, w_a), lambda b, i: (b * nq + i, 0)),
                  pl.BlockSpec((t, w_a), lambda b, i: (b, 0)),
                  pl.BlockSpec((t, w_a), lambda b, i: (b, 0)),
                  pl.BlockSpec((tq, LANE), lambda b, i: (b * nq + i, 0)),
                  pl.BlockSpec((1, nh, t), lambda b, i: (b, 0, 0))],
        out_specs=pl.BlockSpec((tq, w_a), lambda b, i: (b * nq + i, 0)),
        out_shape=jax.ShapeDtypeStruct((n, w_a), bf16),
        scratch_shapes=[pltpu.VMEM((nh, tq, LANE), f32)] * 4,
        compiler_params=_params(("parallel", "arbitrary"), 40),
        name="fox_attn",
    )(qa, kab, vab, c, ct)


def _gla_kernel(q_ref, k_ref, v_ref, r_ref, fg_ref, wgk_ref, bgk_ref, gout_ref, s0_ref,
                o_ref, st_ref, s_scr, *, nh, dk, dv, chunk, n_chunks, n_seq, t_valid):
    t = pl.program_id(1)

    @pl.when(t == 0)
    def _():
        s_scr[...] = s0_ref[...]

    tri = _tri(chunk)
    row = lax.broadcasted_iota(jnp.int32, (chunk, chunk), 0)
    col = lax.broadcasted_iota(jnp.int32, (chunk, chunk), 1)
    causal = col <= row
    items = [(e, ci) for e in range(n_seq) for ci in range(n_chunks)]
    rows_of = lambda it: slice((it[0] * n_chunks + it[1]) * chunk, (it[0] * n_chunks + it[1] + 1) * chunk)
    heads = [(slice(h * dk, (h + 1) * dk), slice(h * dv, (h + 1) * dv)) for h in range(nh)]

    def stage1(it):
        la = _log_sigmoid(_dot(fg_ref[rows_of(it), :].astype(bf16), wgk_ref[...]) + bgk_ref[...]) / GK_NORM
        if t_valid is not None:
            la = jnp.where(lax.broadcasted_iota(jnp.int32, la.shape, 0) < t_valid, la, 0.0)
        return la

    def stage2(la):
        b = _dot_sel_left(tri, la)
        b_last = b[chunk - 1:chunk, :]
        return jnp.exp(b), jnp.exp(-b), jnp.exp(b_last - b), jnp.exp(b_last)

    def stage3(it, exps):
        e_pos, e_neg, e_rem, e_last = exps
        rows = rows_of(it)
        out = []
        for kl, _ in heads:
            q = q_ref[rows, kl].astype(f32)
            k = k_ref[rows, kl].astype(f32)
            qe = (q * e_pos[:, kl]).astype(bf16)
            ke = (k * e_neg[:, kl]).astype(bf16)
            kd = (k * e_rem[:, kl]).astype(bf16)
            a = lax.dot_general(qe, ke, _NT, preferred_element_type=f32)
            a = jnp.where(causal, a, 0.0).astype(bf16)
            decay = jnp.broadcast_to(e_last[:, kl], (SUBLANE, dk)).T[:, :1]
            out.append((qe, kd, a, decay))
        return out

    def stage4(it, per_head):
        rows = rows_of(it)
        for h, ((_, vl), (qe, kd, a, decay)) in enumerate(zip(heads, per_head)):
            v = v_ref[rows, vl]
            s = s_scr[it[0], h]
            o = _dot(qe, s.astype(bf16)) + _dot(a, v)
            s_scr[it[0], h] = s * decay + lax.dot_general(kd, v, _TN, preferred_element_type=f32)
            gated = _rms(o, gout_ref[...]) * jax.nn.silu(r_ref[rows, vl].astype(f32))
            o_ref[rows, vl] = gated.astype(bf16)

    r1, r2, r3 = {}, {}, {}
    for tick in range(len(items) + 3):
        if 0 <= tick - 3 < len(items):
            stage4(items[tick - 3], r3.pop(tick - 3))
        if 0 <= tick - 2 < len(items):
            r3[tick - 2] = stage3(items[tick - 2], r2.pop(tick - 2))
        if 0 <= tick - 1 < len(items):
            r2[tick - 1] = stage2(r1.pop(tick - 1))
        if tick < len(items):
            r1[tick] = stage1(items[tick])

    @pl.when(t == pl.num_programs(1) - 1)
    def _():
        st_ref[...] = s_scr[...]


def _gla(qb, kb, vb, rb, fg, wgk_pad, bgk, gout, s0, *, batch, tc, n_seq, chunk, t_valid):
    n, wk_b = qb.shape
    wv_b = vb.shape[1]
    _, nh, dk, dv = s0.shape
    nt = n // batch // tc
    assert n_seq == 1 or nt == 1
    rowspec = lambda width: pl.BlockSpec((n_seq * tc, width), lambda b, t: (b * nt + t, 0))
    state = pl.BlockSpec((n_seq, nh, dk, dv), lambda b, t: (b, 0, 0, 0))
    kern = functools.partial(_gla_kernel, nh=nh, dk=dk, dv=dv, chunk=chunk, n_chunks=tc // chunk, n_seq=n_seq,
                             t_valid=t_valid)
    return pl.pallas_call(
        kern,
        grid=(batch // n_seq, nt),
        in_specs=[rowspec(wk_b), rowspec(wk_b), rowspec(wv_b), rowspec(wv_b), rowspec(LANE),
                  _resident((LANE, wk_b)), _resident((1, wk_b)), _resident((1, dv)), state],
        out_specs=[rowspec(wv_b), state],
        out_shape=[jax.ShapeDtypeStruct((n, wv_b), bf16), jax.ShapeDtypeStruct(s0.shape, f32)],
        scratch_shapes=[pltpu.VMEM((n_seq, nh, dk, dv), f32)],
        compiler_params=_params(("parallel", "arbitrary"), 32),
        name="gla",
    )(qb, kb, vb, rb, fg, wgk_pad, bgk, gout, s0)


def _outproj_kernel(oa_ref, ob_ref, x_ref, w_ref, x1_ref, *, w_a):
    mixed = _dot(oa_ref[...], w_ref[:w_a, :]) + _dot(ob_ref[...], w_ref[w_a:, :])
    x1_ref[...] = x_ref[...] + mixed


def _outproj(oa, ob, x, w_out, *, tm):
    n, d = x.shape
    w_a, wv_b = oa.shape[1], ob.shape[1]
    row = lambda width: pl.BlockSpec((tm, width), lambda i: (i, 0))
    return pl.pallas_call(
        functools.partial(_outproj_kernel, w_a=w_a),
        grid=(n // tm,),
        in_specs=[row(w_a), row(wv_b), row(d), _resident((w_a + wv_b, d))],
        out_specs=row(d),
        out_shape=jax.ShapeDtypeStruct((n, d), f32),
        compiler_params=_params(("parallel",), 40),
        name="outproj",
    )(oa, ob, x, w_out)


def _ffn_kernel(x_ref, g_ref, wup_ref, wdn_ref, o_ref, h_scr):
    @pl.when(pl.program_id(1) == 0)
    def _():
        x = x_ref[...]
        h_scr[...] = _rms(x, g_ref[...]).astype(bf16)
        o_ref[...] = x

    a = _dot(h_scr[...], wup_ref[...])
    a = jnp.square(jnp.maximum(a, 0.0)).astype(bf16)
    o_ref[...] += _dot(a, wdn_ref[...])


def _ffn(x, g, w_up, w_down, *, tm, tf):
    n, d = x.shape
    dff = w_up.shape[1]
    return pl.pallas_call(
        _ffn_kernel,
        grid=(n // tm, dff // tf),
        in_specs=[pl.BlockSpec((tm, d), lambda i, f: (i, 0)), _resident((1, d)),
                  pl.BlockSpec((d, tf), lambda i, f: (0, f)),
                  pl.BlockSpec((tf, d), lambda i, f: (f, 0))],
        out_specs=pl.BlockSpec((tm, d), lambda i, f: (i, 0)),
        out_shape=jax.ShapeDtypeStruct((n, d), f32),
        scratch_shapes=[pltpu.VMEM((tm, d), bf16)],
        compiler_params=_params(("parallel", "arbitrary"), 52),
        name="ffn",
    )(x, g, w_up, w_down)


def _ple_kernel(x_ref, p_ref, wple_ref, gple_ref, ggate_ref, wgate_ref, gfin_ref, y_ref, *, final):
    x = x_ref[...]
    e = _rms(_dot(p_ref[...].astype(bf16), wple_ref[...]), gple_ref[...])
    gate = jax.nn.sigmoid(_dot(_rms(x, ggate_ref[...]).astype(bf16), wgate_ref[...]))
    x = x + gate * e
    y_ref[...] = _rms(x, gfin_ref[...]) if final else x


def _ple(x, p, w_ple, g_ple, g_gate, w_gate, g_final, *, tm, final):
    n, d = x.shape
    pd = p.shape[1]
    row = lambda width: pl.BlockSpec((tm, width), lambda i: (i, 0))
    return pl.pallas_call(
        functools.partial(_ple_kernel, final=final),
        grid=(n // tm,),
        in_specs=[row(d), row(pd), _resident((pd, d)), _resident((1, d)), _resident((1, d)),
                  _resident((d, d)), _resident((1, d))],
        out_specs=row(d),
        out_shape=jax.ShapeDtypeStruct((n, d), f32),
        compiler_params=_params(("parallel",), 40),
        name="ple",
    )(x, p, w_ple, g_ple, g_gate, w_gate, g_final)


def _scan_tokens(x, nh, n_tokens):
    lane = lax.broadcasted_iota(jnp.int32, x.shape, 1)
    shift = nh
    while shift < n_tokens * nh:
        x = x + jnp.where(lane >= shift, pltpu.roll(x, shift, axis=1), 0.0)
        shift *= 2
    return x


def _spread_last_group(x, nh):
    shift = nh
    while shift < LANE:
        x = x + pltpu.roll(x, shift, axis=1)
        shift *= 2
    return x


def _fox_decode_kernel(pt_ref, q_ref, knew_ref, vnew_ref, lfnew_ref, ck_hbm, cv_hbm, clf_hbm, o_ref,
                       kbuf, vbuf, lbuf, ksem, vsem, lsem, k2, v2, knew_scr, vnew_scr,
                       *, nh, hd, ts, n_pages, ppc):
    b = pl.program_id(0)
    prow = lbuf.shape[2]
    page = prow // nh
    hp = ppc // 2
    half = hp * prow
    rows_q = ts * nh

    def page_copies(hbm, buf, sem, rows, bb, slot):
        return [pltpu.make_async_copy(hbm.at[pt_ref[bb, p]], buf.at[slot, pl.ds(p * rows, rows)], sem.at[slot])
                for p in range(n_pages)]

    def all_copies(bb, slot):
        return (page_copies(ck_hbm, kbuf, ksem, prow, bb, slot) + page_copies(cv_hbm, vbuf, vsem, prow, bb, slot)
                + page_copies(clf_hbm, lbuf, lsem, 1, bb, slot))

    @pl.when(b == 0)
    def _():
        for cp in all_copies(0, 0):
            cp.start()
        knew_scr[...] = jnp.zeros(knew_scr.shape, bf16)
        vnew_scr[...] = jnp.zeros(vnew_scr.shape, bf16)

    @pl.when(b + 1 < pl.num_programs(0))
    def _():
        for cp in all_copies(b + 1, (b + 1) % 2):
            cp.start()

    slot = b % 2
    for cp in all_copies(b, slot):
        cp.wait()

    within = _scan_tokens(lbuf[slot], nh, page)
    lane = lax.broadcasted_iota(jnp.int32, (n_pages, LANE), 1)
    totals = jnp.where(lane >= LANE - nh, within[:, prow - LANE:], 0.0)
    pr = lax.broadcasted_iota(jnp.int32, (n_pages, n_pages), 0)
    pc = lax.broadcasted_iota(jnp.int32, (n_pages, n_pages), 1)
    before = _dot_sel_left((pr > pc).astype(bf16), totals)
    cflat = within + jnp.concatenate([_spread_last_group(before, nh)] * (prow // LANE), axis=1)
    whole = _spread_last_group(before[n_pages - 1:, :] + totals[n_pages - 1:, :], nh)
    c_new = _scan_tokens(lfnew_ref[0], nh, LANE // nh) + whole
    rr = lax.broadcasted_iota(jnp.int32, (rows_q, LANE), 0)
    ll = lax.broadcasted_iota(jnp.int32, (rows_q, LANE), 1)
    cq = jnp.sum(jnp.where(ll == rr, jnp.broadcast_to(c_new, (rows_q, LANE)), 0.0),
                 axis=-1, keepdims=True)
    q = q_ref[0]
    zero = jnp.zeros(q.shape, bf16)
    q2 = jnp.concatenate([jnp.concatenate([q, zero], axis=1), jnp.concatenate([zero, q], axis=1)], axis=0)

    def both(x):
        return jnp.concatenate([x, x], axis=0)

    m = jnp.full((rows_q, 1), -jnp.inf, f32)
    l = jnp.zeros((rows_q, 1), f32)
    acc = jnp.zeros((2 * rows_q, 2 * hd), f32)
    for blk in range(n_pages // ppc):
        row_a = blk * ppc * prow
        k2[blk, :, :hd] = kbuf[slot, row_a:row_a + half].astype(bf16)
        k2[blk, :, hd:] = kbuf[slot, row_a + half:row_a + 2 * half].astype(bf16)
        s = lax.dot_general(q2, k2[blk], _NT, preferred_element_type=f32)
        pg = blk * ppc
        bias = jnp.concatenate(
            [jnp.concatenate([jnp.broadcast_to(cflat[pg + p:pg + p + 1, :], (rows_q, prow)),
                              jnp.broadcast_to(cflat[pg + hp + p:pg + hp + p + 1, :], (rows_q, prow))], axis=0)
             for p in range(hp)], axis=1)
        r = lax.broadcasted_iota(jnp.int32, s.shape, 0)
        n = lax.broadcasted_iota(jnp.int32, s.shape, 1)
        t = jnp.where(n % nh == r % nh, s - bias, NEG)
        rowmax = jnp.max(t, axis=-1, keepdims=True)
        m_new = jnp.maximum(m, jnp.maximum(rowmax[:rows_q], rowmax[rows_q:]) + cq)
        alpha = jnp.exp(m - m_new)
        p = jnp.exp(t + both(cq - m_new))
        rowsum = jnp.sum(p, axis=-1, keepdims=True)
        l = alpha * l + rowsum[:rows_q] + rowsum[rows_q:]
        m = m_new
        v2[blk, :, :hd] = vbuf[slot, row_a:row_a + half].astype(bf16)
        v2[blk, :, hd:] = vbuf[slot, row_a + half:row_a + 2 * half].astype(bf16)
        acc = both(alpha) * acc + _dot(p.astype(bf16), v2[blk])

    knew_scr[:rows_q, :] = knew_ref[0]
    vnew_scr[:rows_q, :] = vnew_ref[0]
    s = lax.dot_general(q, knew_scr[...], _NT, preferred_element_type=f32)
    r = lax.broadcasted_iota(jnp.int32, s.shape, 0)
    n = lax.broadcasted_iota(jnp.int32, s.shape, 1)
    valid = jnp.logical_and(n % nh == r % nh, n // nh <= r // nh)
    t = jnp.where(valid, s - c_new, NEG)
    m_new = jnp.maximum(m, jnp.max(t, axis=-1, keepdims=True) + cq)
    alpha = jnp.exp(m - m_new)
    p = jnp.exp(t + (cq - m_new))
    l = alpha * l + jnp.sum(p, axis=-1, keepdims=True)
    acc = both(alpha) * acc
    o = acc[:rows_q, :hd] + acc[rows_q:, hd:] + _dot(p.astype(bf16), vnew_scr[...])
    o_ref[0] = o / l


def _fox_decode(page_table, q, knew, vnew, lfnew, cache_k, cache_v, cache_lf, *, nh, ppc):
    bs, rows_q, hd = q.shape
    ts = rows_q // nh
    n_pages = page_table.shape[1]
    prow = cache_k.shape[1]
    assert rows_q <= LANE and hd == LANE and prow % LANE == 0 and LANE % nh == 0
    assert ppc % 2 == 0 and n_pages % ppc == 0
    any_spec = pl.BlockSpec(memory_space=pl.ANY)
    per_b = lambda shape: pl.BlockSpec((1,) + shape, lambda s, pt: (s, 0, 0))
    half = ppc // 2 * prow
    grid_spec = pltpu.PrefetchScalarGridSpec(
        num_scalar_prefetch=1,
        grid=(bs,),
        in_specs=[per_b((rows_q, hd)), per_b((rows_q, hd)), per_b((rows_q, hd)), per_b((1, LANE)),
                  any_spec, any_spec, any_spec],
        out_specs=per_b((rows_q, hd)),
        scratch_shapes=[
            pltpu.VMEM((2, n_pages * prow, hd), f32), pltpu.VMEM((2, n_pages * prow, hd), f32),
            pltpu.VMEM((2, n_pages, prow), f32),
            pltpu.SemaphoreType.DMA((2,)), pltpu.SemaphoreType.DMA((2,)), pltpu.SemaphoreType.DMA((2,)),
            pltpu.VMEM((n_pages // ppc, half, 2 * hd), bf16), pltpu.VMEM((n_pages // ppc, half, 2 * hd), bf16),
            pltpu.VMEM((LANE, hd), bf16), pltpu.VMEM((LANE, hd), bf16),
        ],
    )
    kern = functools.partial(_fox_decode_kernel, nh=nh, hd=hd, ts=ts, n_pages=n_pages, ppc=ppc)
    return pl.pallas_call(
        kern,
        grid_spec=grid_spec,
        out_shape=jax.ShapeDtypeStruct((bs, rows_q, hd), f32),
        compiler_params=_params(("arbitrary",), 56),
        name="fox_decode",
    )(page_table, q, knew, vnew, lfnew, cache_k, cache_v, cache_lf)


def _tile(n, pref):
    while n % pref:
        pref //= 2
    return pref


def kernel(x_prompt, x_sample, cache_k, cache_v, cache_logf, state_gla, page_table, p_prompt, p_sample,
           g_mix, w_in, b_f, w_gk2, b_gk, g_gla_out, w_out, g_mlp, w_up, w_down, w_ple, g_ple, g_ple_gate,
           w_ple_gate, g_final):
    batch, seq, d = x_prompt.shape
    bs, ts, _ = x_sample.shape
    depth = g_mix.shape[0]
    _, n_pool, page, nh, hd = cache_k.shape
    _, _, nh_b, dk, dv = state_gla.shape
    w_a, wk_b, wv_b = nh * hd, nh_b * dk, nh_b * dv
    rank = w_gk2.shape[1]
    n_p, n_s = batch * seq, bs * ts
    assert ts <= SUBLANE and ts % GLA_CHUNK != 0 and seq % GLA_CHUNK == 0
    row = lambda a: a.reshape(1, -1)

    xp = x_prompt.reshape(n_p, d)
    xs = x_sample.reshape(n_s, d)
    outs = {k: [] for k in ("kp", "vp", "fp", "sp", "ks", "vs", "fs", "ss")}
    for l in range(depth):
        final = l == depth - 1
        wl = w_in[l]
        o_f = 3 * w_a
        o_b = o_f + nh
        o_g = o_b + 2 * wk_b + 2 * wv_b
        w_main = jnp.concatenate([wl[:, :o_f], wl[:, o_b:o_g]], axis=1).astype(bf16)
        w_fg = jnp.concatenate([wl[:, o_f:o_b], wl[:, o_g:], jnp.zeros((d, LANE - nh - rank), f32)], axis=1).astype(bf16)
        bf_pad = jnp.zeros((1, LANE), f32).at[0, :nh].set(b_f[l])
        wgk_pad = jnp.zeros((LANE, wk_b), f32).at[nh:nh + rank].set(w_gk2[l]).astype(bf16)
        w_out_b, w_up_b, w_down_b = w_out[l].astype(bf16), w_up[l].astype(bf16), w_down[l].astype(bf16)
        w_ple_b, w_gate_b = w_ple[l].astype(bf16), w_ple_gate[l].astype(bf16)
        inproj = functools.partial(_inproj, g=row(g_mix[l]), w_main=w_main, w_fg=w_fg, nh=nh, hd=hd, wk_b=wk_b,
                                   wv_b=wv_b, scale_a=hd ** -0.5, scale_b=dk ** -0.5)
        gla = functools.partial(_gla, wgk_pad=wgk_pad, bgk=row(b_gk[l]), gout=row(g_gla_out[l]))

        def tail(x, oa, ob, p, n):
            x = _outproj(oa, ob, x, w_out_b, tm=_tile(n, 512))
            x = _ffn(x, row(g_mlp[l]), w_up_b, w_down_b, tm=_tile(n, 1024), tf=512)
            return _ple(x, p.reshape(n, -1), w_ple_b, row(g_ple[l]), row(g_ple_gate[l]), w_gate_b,
                        row(g_final), tm=_tile(n, 512), final=final)

        qa, kab, vab, qb, kb, vb, rb, ka, va, fg = inproj(xp, tm=_tile(n_p, 256))
        logf, c, ct = _fox_prep(fg, bf_pad, batch=batch, nh=nh)
        oa = _fox_attn(qa, kab, vab, c, ct, batch=batch, nh=nh, hd=hd, tq=_tile(seq, 256))
        ob, s_p = gla(qb, kb, vb, rb, fg, s0=jnp.zeros((batch, nh_b, dk, dv), f32), batch=batch,
                      tc=_tile(seq, 1024), n_seq=1, chunk=GLA_CHUNK, t_valid=None)
        xp = tail(xp, oa, ob, p_prompt[l], n_p)
        outs["kp"].append(ka.reshape(batch, seq, nh, hd))
        outs["vp"].append(va.reshape(batch, seq, nh, hd))
        outs["fp"].append(logf.reshape(batch, seq, nh))
        outs["sp"].append(s_p)

        qa, kab, vab, qb, kb, vb, rb, ka, va, fg = inproj(xs, tm=_tile(n_s, 256))
        logf, _, _ = _fox_prep(fg, bf_pad, batch=1, nh=nh)
        key_rows = lambda a: a.reshape(bs, ts * nh, hd)
        lfnew = jnp.pad(logf.reshape(bs, 1, ts * nh), ((0, 0), (0, 0), (0, LANE - ts * nh)))
        oa = _fox_decode(page_table, key_rows(qa), key_rows(kab), key_rows(vab), lfnew,
                         cache_k[l].reshape(n_pool, page * nh, hd), cache_v[l].reshape(n_pool, page * nh, hd),
                         cache_logf[l].reshape(n_pool, 1, page * nh), nh=nh, ppc=4)
        oa = oa.reshape(n_s, w_a).astype(bf16)
        tpad = 16
        flat = lambda a: jnp.pad(a.reshape(bs, ts, -1), ((0, 0), (0, tpad - ts), (0, 0))).reshape(bs * tpad, -1)
        ob, s_s = gla(flat(qb), flat(kb), flat(vb), flat(rb), flat(fg), s0=state_gla[l], batch=bs,
                      tc=tpad, n_seq=_tile(bs, 8), chunk=tpad, t_valid=ts)
        ob = ob.reshape(bs, tpad, wv_b)[:, :ts].reshape(n_s, wv_b)
        xs = tail(xs, oa, ob, p_sample[l], n_s)
        outs["ks"].append(ka.reshape(bs, ts, nh, hd))
        outs["vs"].append(va.reshape(bs, ts, nh, hd))
        outs["fs"].append(logf.reshape(bs, ts, nh))
        outs["ss"].append(s_s)

    st = lambda key: jnp.stack(outs[key])
    return (xp.reshape(batch, seq, d), xs.reshape(bs, ts, d), st("kp"), st("vp"), st("fp"), st("sp"),
            st("ks"), st("vs"), st("fs"), st("ss"))
```

```python
import functools

import jax
import jax.numpy as jnp
from jax import lax
from jax.experimental import pallas as pl
from jax.experimental.pallas import tpu as pltpu

f32 = jnp.float32
bf16 = jnp.bfloat16

EPS = 1e-6
NEG = -1e30
GK_NORM = 16.0
GLA_CHUNK = 64
LOG2E = 1.4426950408889634

LANE = 128
SUBLANE = 8
MIB = 2**20

_NT = (((1,), (1,)), ((), ()))
_TN = (((0,), (0,)), ((), ()))


def _params(semantics, vmem_mib):
    return pltpu.CompilerParams(dimension_semantics=semantics, vmem_limit_bytes=vmem_mib * MIB)


def _resident(shape):
    return pl.BlockSpec(shape, lambda *_: (0,) * len(shape), pipeline_mode=pl.Buffered(1))


def _rms(x, g):
    return x * lax.rsqrt(jnp.mean(x * x, axis=-1, keepdims=True) + EPS) * g


def _log_sigmoid(x):
    return jnp.minimum(x, 0.0) - jnp.log1p(jnp.exp(-jnp.abs(x)))


def _split3(x):
    hi = x.astype(bf16)
    r = x - hi.astype(f32)
    mid = r.astype(bf16)
    lo = (r - mid.astype(f32)).astype(bf16)
    return hi, mid, lo


def _dot(a, b):
    return jnp.dot(a, b, preferred_element_type=f32)


def _dot_sel_left(sel, x):
    return sum(_dot(sel, t) for t in _split3(x))


def _tri(n):
    r = lax.broadcasted_iota(jnp.int32, (n, n), 0)
    c = lax.broadcasted_iota(jnp.int32, (n, n), 1)
    return (r >= c).astype(bf16)


def _inproj_kernel(x_ref, g_ref, wa_ref, wb_ref, wfg_ref,
                   qa_ref, kab_ref, vab_ref, qb_ref, kb_ref, vb_ref, rb_ref, ka_ref, va_ref, fg_ref,
                   *, w_a, wk_b, wv_b, scale_a, scale_b, chunk):
    h = _rms(x_ref[...], g_ref[...]).astype(bf16)

    def section(w_ref, col, width, store):
        for j in range(0, width, chunk):
            store(j, _dot(h, w_ref[:, col + j:col + j + chunk]))

    def st_qa(j, z):
        qa_ref[:, j:j + chunk] = (z * scale_a).astype(bf16)

    def st_ka(j, z):
        ka_ref[:, j:j + chunk] = z
        kab_ref[:, j:j + chunk] = z.astype(bf16)

    def st_va(j, z):
        va_ref[:, j:j + chunk] = z
        vab_ref[:, j:j + chunk] = z.astype(bf16)

    def st_qb(j, z):
        qb_ref[:, j:j + chunk] = (z * scale_b).astype(bf16)

    def st_kb(j, z):
        kb_ref[:, j:j + chunk] = z.astype(bf16)

    def st_vb(j, z):
        vb_ref[:, j:j + chunk] = z.astype(bf16)

    def st_rb(j, z):
        rb_ref[:, j:j + chunk] = z.astype(bf16)

    section(wa_ref, 0, w_a, st_qa)
    section(wa_ref, w_a, w_a, st_ka)
    section(wa_ref, 2 * w_a, w_a, st_va)
    section(wb_ref, 0, wk_b, st_qb)
    section(wb_ref, wk_b, wk_b, st_kb)
    section(wb_ref, 2 * wk_b, wv_b, st_vb)
    section(wb_ref, 2 * wk_b + wv_b, wv_b, st_rb)
    fg_ref[...] = _dot(h, wfg_ref[...])


def _inproj(x, g, w_qkv, w_gla, w_fg, *, nh, hd, wk_b, wv_b, scale_a, scale_b, tm):
    n, d = x.shape
    w_a = nh * hd
    row = lambda width: pl.BlockSpec((tm, width), lambda i: (i, 0))
    sds = jax.ShapeDtypeStruct
    kern = functools.partial(_inproj_kernel, w_a=w_a, wk_b=wk_b, wv_b=wv_b,
                             scale_a=scale_a, scale_b=scale_b, chunk=512)
    return pl.pallas_call(
        kern,
        grid=(n // tm,),
        in_specs=[row(d), _resident((1, d)), _resident(w_qkv.shape), _resident(w_gla.shape), _resident((d, LANE))],
        out_specs=[row(w_a), row(w_a), row(w_a), row(wk_b), row(wk_b), row(wv_b), row(wv_b),
                   row(w_a), row(w_a), row(LANE)],
        out_shape=[sds((n, w_a), bf16), sds((n, w_a), bf16), sds((n, w_a), bf16),
                   sds((n, wk_b), bf16), sds((n, wk_b), bf16), sds((n, wv_b), bf16), sds((n, wv_b), bf16),
                   sds((n, w_a), f32), sds((n, w_a), f32), sds((n, LANE), f32)],
        compiler_params=_params(("parallel",), 52),
        name="inproj",
    )(x, g, w_qkv, w_gla, w_fg)


def _fox_prep_kernel(fg_ref, bf_ref, logf_ref, c_ref, ct_ref, *, nh, blk):
    t = fg_ref.shape[0]
    tri = _tri(blk)
    carry = jnp.zeros((1, LANE), f32)
    for b in range(t // blk):
        rows = slice(b * blk, (b + 1) * blk)
        logf = _log_sigmoid(fg_ref[rows, :] + bf_ref[...])
        logf_ref[rows, :] = logf[:, :nh]
        c = _dot_sel_left(tri, logf) + carry
        carry = c[blk - 1:blk, :]
        c2 = c * LOG2E
        c_ref[rows, :] = c2
        ct_ref[0, :, rows] = c2.T[:nh, :]


def _fox_prep(fg, bf_pad, *, batch, nh):
    n = fg.shape[0]
    t = n // batch
    blk = min(t, 256)
    sds = jax.ShapeDtypeStruct
    return pl.pallas_call(
        functools.partial(_fox_prep_kernel, nh=nh, blk=blk),
        grid=(batch,),
        in_specs=[pl.BlockSpec((t, LANE), lambda b: (b, 0)), _resident((1, LANE))],
        out_specs=[pl.BlockSpec((t, nh), lambda b: (b, 0)),
                   pl.BlockSpec((t, LANE), lambda b: (b, 0)),
                   pl.BlockSpec((1, nh, t), lambda b: (b, 0, 0))],
        out_shape=[sds((n, nh), f32), sds((n, LANE), f32), sds((batch, nh, t), f32)],
        compiler_params=_params(("parallel",), 32),
        name="fox_prep",
    )(fg, bf_pad)


def _fox_attn_kernel(q_ref, k_ref, v_ref, c_ref, ct_ref, o_ref, m_scr, l_scr, acc_scr, cq_scr, *, nh, hd, tq):
    i = pl.program_id(1)
    row = lax.broadcasted_iota(jnp.int32, (tq, tq), 0)
    col = lax.broadcasted_iota(jnp.int32, (tq, tq), 1)
    causal = col <= row
    m_scr[...] = jnp.full(m_scr.shape, -jnp.inf, f32)
    l_scr[...] = jnp.zeros(l_scr.shape, f32)
    acc_scr[...] = jnp.zeros(acc_scr.shape, f32)
    for h in range(nh):
        cq_scr[h] = jnp.broadcast_to(c_ref[:, h:h + 1], (tq, LANE))

    def weights(h, t):
        m, cq = m_scr[h], cq_scr[h]
        m_new = jnp.maximum(m, jnp.max(t, axis=-1, keepdims=True) + cq)
        alpha = jnp.exp2(m - m_new)
        p = jnp.exp2(t + jnp.concatenate([cq - m_new] * (tq // LANE), axis=1))
        l_scr[h] = alpha * l_scr[h] + jnp.sum(p, axis=-1, keepdims=True)
        m_scr[h] = m_new
        return p.astype(bf16), alpha

    def block(j, diag):
        keys = pl.ds(pl.multiple_of(j * tq, tq), tq)
        for h0 in range(0, nh, 2):
            ps, alphas = [], []
            for h in (h0, h0 + 1):
                lanes = slice(h * hd, (h + 1) * hd)
                t = lax.dot_general(q_ref[:, lanes], k_ref[keys, lanes], _NT, preferred_element_type=f32)
                t = t - ct_ref[0, h:h + 1, keys]
                if diag:
                    t = jnp.where(causal, t, NEG)
                p, alpha = weights(h, t)
                ps.append(p)
                alphas.append(alpha)
            pv = _dot(jnp.concatenate(ps, axis=0), v_ref[keys, h0 * hd:(h0 + 2) * hd])
            acc_scr[h0] = alphas[0] * acc_scr[h0] + pv[:tq, :hd]
            acc_scr[h0 + 1] = alphas[1] * acc_scr[h0 + 1] + pv[tq:, hd:]

    def off_diagonal(j, carry):
        block(j, False)
        return carry

    lax.fori_loop(0, i, off_diagonal, 0)
    block(i, True)
    for h in range(nh):
        o_ref[:, h * hd:(h + 1) * hd] = (acc_scr[h] / l_scr[h]).astype(bf16)


def _fox_attn(qa, kab, vab, c, ct, *, batch, nh, hd, tq):
    n, w_a = qa.shape
    t = n // batch
    nq = t // tq
    assert hd == LANE and nh % 2 == 0 and tq % LANE == 0
    return pl.pallas_call(
        functools.partial(_fox_attn_kernel, nh=nh, hd=hd, tq=tq),
        grid=(batch, nq),
        in_specs=[pl.BlockSpec((tq, w_a), lambda b, i: (b * nq + i, 0)),
                  pl.BlockSpec((t, w_a), lambda b, i: (b, 0)),
                  pl.BlockSpec((t, w_a), lambda b, i: (b, 0)),
                  pl.BlockSpec((tq, LANE), lambda b, i: (b * nq + i, 0)),
                  pl.BlockSpec((1, nh, t), lambda b, i: (b, 0, 0))],
        out_specs=pl.BlockSpec((tq, w_a), lambda b, i: (b * nq + i, 0)),
        out_shape=jax.ShapeDtypeStruct((n, w_a), bf16),
        scratch_shapes=[pltpu.VMEM((nh, tq, LANE), f32)] * 4,
        compiler_params=_params(("parallel", "arbitrary"), 40),
        name="fox_attn",
    )(qa, kab, vab, c, ct)


def _gla_kernel(q_ref, k_ref, v_ref, r_ref, fg_ref, wgk_ref, bgk_ref, gout_ref, s0_ref,
                o_ref, st_ref, s_scr, *, nh, dk, dv, chunk, n_chunks, n_seq, t_valid):
    t = pl.program_id(1)

    @pl.when(t == 0)
    def _():
        s_scr[...] = s0_ref[...]

    tri = _tri(chunk)
    row = lax.broadcasted_iota(jnp.int32, (chunk, chunk), 0)
    col = lax.broadcasted_iota(jnp.int32, (chunk, chunk), 1)
    causal = col <= row
    items = [(e, ci) for e in range(n_seq) for ci in range(n_chunks)]
    rows_of = lambda it: slice((it[0] * n_chunks + it[1]) * chunk, (it[0] * n_chunks + it[1] + 1) * chunk)
    heads = [(slice(h * dk, (h + 1) * dk), slice(h * dv, (h + 1) * dv)) for h in range(nh)]

    def stage1(it):
        la = _log_sigmoid(_dot(fg_ref[rows_of(it), :].astype(bf16), wgk_ref[...]) + bgk_ref[...]) / GK_NORM
        if t_valid is not None:
            la = jnp.where(lax.broadcasted_iota(jnp.int32, la.shape, 0) < t_valid, la, 0.0)
        return la

    def stage2(la):
        b = _dot_sel_left(tri, la)
        b_last = b[chunk - 1:chunk, :]
        return jnp.exp(b), jnp.exp(-b), jnp.exp(b_last - b), jnp.exp(b_last)

    def stage3(it, exps):
        e_pos, e_neg, e_rem, e_last = exps
        rows = rows_of(it)
        out = []
        for kl, _ in heads:
            q = q_ref[rows, kl].astype(f32)
            k = k_ref[rows, kl].astype(f32)
            qe = (q * e_pos[:, kl]).astype(bf16)
            ke = (k * e_neg[:, kl]).astype(bf16)
            kd = (k * e_rem[:, kl]).astype(bf16)
            a = lax.dot_general(qe, ke, _NT, preferred_element_type=f32)
            a = jnp.where(causal, a, 0.0).astype(bf16)
            decay = jnp.broadcast_to(e_last[:, kl], (SUBLANE, dk)).T[:, :1]
            out.append((qe, kd, a, decay))
        return out

    def stage4(it, per_head):
        rows = rows_of(it)
        for h, ((_, vl), (qe, kd, a, decay)) in enumerate(zip(heads, per_head)):
            v = v_ref[rows, vl]
            s = s_scr[it[0], h]
            o = _dot(qe, s.astype(bf16)) + _dot(a, v)
            s_scr[it[0], h] = s * decay + lax.dot_general(kd, v, _TN, preferred_element_type=f32)
            gated = _rms(o, gout_ref[...]) * jax.nn.silu(r_ref[rows, vl].astype(f32))
            o_ref[rows, vl] = gated.astype(bf16)

    r1, r2, r3 = {}, {}, {}
    for tick in range(len(items) + 3):
        if 0 <= tick - 3 < len(items):
            stage4(items[tick - 3], r3.pop(tick - 3))
        if 0 <= tick - 2 < len(items):
            r3[tick - 2] = stage3(items[tick - 2], r2.pop(tick - 2))
        if 0 <= tick - 1 < len(items):
            r2[tick - 1] = stage2(r1.pop(tick - 1))
        if tick < len(items):
            r1[tick] = stage1(items[tick])

    @pl.when(t == pl.num_programs(1) - 1)
    def _():
        st_ref[...] = s_scr[...]


def _gla(qb, kb, vb, rb, fg, wgk_pad, bgk, gout, s0, *, batch, tc, n_seq, chunk, t_valid):
    n, wk_b = qb.shape
    wv_b = vb.shape[1]
    _, nh, dk, dv = s0.shape
    nt = n // batch // tc
    assert n_seq == 1 or nt == 1
    rowspec = lambda width: pl.BlockSpec((n_seq * tc, width), lambda b, t: (b * nt + t, 0))
    state = pl.BlockSpec((n_seq, nh, dk, dv), lambda b, t: (b, 0, 0, 0))
    kern = functools.partial(_gla_kernel, nh=nh, dk=dk, dv=dv, chunk=chunk, n_chunks=tc // chunk, n_seq=n_seq,
                             t_valid=t_valid)
    return pl.pallas_call(
        kern,
        grid=(batch // n_seq, nt),
        in_specs=[rowspec(wk_b), rowspec(wk_b), rowspec(wv_b), rowspec(wv_b), rowspec(LANE),
                  _resident((LANE, wk_b)), _resident((1, wk_b)), _resident((1, dv)), state],
        out_specs=[rowspec(wv_b), state],
        out_shape=[jax.ShapeDtypeStruct((n, wv_b), bf16), jax.ShapeDtypeStruct(s0.shape, f32)],
        scratch_shapes=[pltpu.VMEM((n_seq, nh, dk, dv), f32)],
        compiler_params=_params(("parallel", "arbitrary"), 32),
        name="gla",
    )(qb, kb, vb, rb, fg, wgk_pad, bgk, gout, s0)


def _outproj_kernel(oa_ref, ob_ref, x_ref, w_ref, x1_ref, *, w_a):
    mixed = _dot(oa_ref[...], w_ref[:w_a, :]) + _dot(ob_ref[...], w_ref[w_a:, :])
    x1_ref[...] = x_ref[...] + mixed


def _outproj(oa, ob, x, w_out, *, tm):
    n, d = x.shape
    w_a, wv_b = oa.shape[1], ob.shape[1]
    row = lambda width: pl.BlockSpec((tm, width), lambda i: (i, 0))
    return pl.pallas_call(
        functools.partial(_outproj_kernel, w_a=w_a),
        grid=(n // tm,),
        in_specs=[row(w_a), row(wv_b), row(d), _resident((w_a + wv_b, d))],
        out_specs=row(d),
        out_shape=jax.ShapeDtypeStruct((n, d), f32),
        compiler_params=_params(("parallel",), 40),
        name="outproj",
    )(oa, ob, x, w_out)


def _ffn_kernel(x_ref, g_ref, wup_ref, wdn_ref, o_ref, *rest):
    h_scr = rest[-1]

    @pl.when(pl.program_id(1) == 0)
    def _():
        x = x_ref[...]
        h_scr[...] = _rms(x, g_ref[...]).astype(bf16)
        o_ref[...] = x

    w_up, w_down = wup_ref[...].astype(bf16), wdn_ref[...].astype(bf16)
    if len(rest) == 3:
        rest[0][...] = w_up
        rest[1][...] = w_down
    a = _dot(h_scr[...], w_up)
    a = jnp.square(jnp.maximum(a, 0.0)).astype(bf16)
    o_ref[...] += _dot(a, w_down)


def _ffn(x, g, w_up, w_down, *, tm, tf):
    n, d = x.shape
    dff = w_up.shape[1]
    emit_cast = w_up.dtype != bf16
    assert not emit_cast or n == tm
    x_spec = pl.BlockSpec((tm, d), lambda i, f: (i, 0))
    up_spec = pl.BlockSpec((d, tf), lambda i, f: (0, f))
    down_spec = pl.BlockSpec((tf, d), lambda i, f: (f, 0))
    sds = jax.ShapeDtypeStruct
    out = pl.pallas_call(
        _ffn_kernel,
        grid=(n // tm, dff // tf),
        in_specs=[x_spec, _resident((1, d)), up_spec, down_spec],
        out_specs=[x_spec] + ([up_spec, down_spec] if emit_cast else []),
        out_shape=[sds((n, d), f32)] + ([sds(w_up.shape, bf16), sds(w_down.shape, bf16)] if emit_cast else []),
        scratch_shapes=[pltpu.VMEM((tm, d), bf16)],
        compiler_params=_params(("parallel", "arbitrary"), 52),
        name="ffn",
    )(x, g, w_up, w_down)
    return out if emit_cast else out[0]


def _ple_kernel(x_ref, p_ref, wple_ref, gple_ref, ggate_ref, wgate_ref, gfin_ref, y_ref, *, final):
    x = x_ref[...]
    e = _rms(_dot(p_ref[...].astype(bf16), wple_ref[...]), gple_ref[...])
    gate = jax.nn.sigmoid(_dot(_rms(x, ggate_ref[...]).astype(bf16), wgate_ref[...]))
    x = x + gate * e
    y_ref[...] = _rms(x, gfin_ref[...]) if final else x


def _ple(x, p, w_ple, g_ple, g_gate, w_gate, g_final, *, tm, final):
    n, d = x.shape
    pd = p.shape[1]
    row = lambda width: pl.BlockSpec((tm, width), lambda i: (i, 0))
    return pl.pallas_call(
        functools.partial(_ple_kernel, final=final),
        grid=(n // tm,),
        in_specs=[row(d), row(pd), _resident((pd, d)), _resident((1, d)), _resident((1, d)),
                  _resident((d, d)), _resident((1, d))],
        out_specs=row(d),
        out_shape=jax.ShapeDtypeStruct((n, d), f32),
        compiler_params=_params(("parallel",), 40),
        name="ple",
    )(x, p, w_ple, g_ple, g_gate, w_gate, g_final)


def _scan_tokens(x, nh, n_tokens):
    lane = lax.broadcasted_iota(jnp.int32, x.shape, 1)
    shift = nh
    while shift < n_tokens * nh:
        x = x + jnp.where(lane >= shift, pltpu.roll(x, shift, axis=1), 0.0)
        shift *= 2
    return x


def _spread_last_group(x, nh):
    shift = nh
    while shift < LANE:
        x = x + pltpu.roll(x, shift, axis=1)
        shift *= 2
    return x


def _fox_decode_kernel(pt_ref, q_ref, knew_ref, vnew_ref, lfnew_ref, ck_hbm, cv_hbm, clf_hbm, o_ref,
                       kbuf, vbuf, lbuf, ksem, vsem, lsem, k2, v2, knew_scr, vnew_scr,
                       *, nh, hd, ts, n_pages, ppc):
    b = pl.program_id(0)
    prow = lbuf.shape[2]
    page = prow // nh
    hp = ppc // 2
    half = hp * prow
    rows_q = ts * nh

    def page_copies(hbm, buf, sem, rows, bb, slot):
        return [pltpu.make_async_copy(hbm.at[pt_ref[bb, p]], buf.at[slot, pl.ds(p * rows, rows)], sem.at[slot])
                for p in range(n_pages)]

    def all_copies(bb, slot):
        return (page_copies(ck_hbm, kbuf, ksem, prow, bb, slot) + page_copies(cv_hbm, vbuf, vsem, prow, bb, slot)
                + page_copies(clf_hbm, lbuf, lsem, 1, bb, slot))

    @pl.when(b == 0)
    def _():
        for cp in all_copies(0, 0):
            cp.start()
        knew_scr[...] = jnp.zeros(knew_scr.shape, bf16)
        vnew_scr[...] = jnp.zeros(vnew_scr.shape, bf16)

    @pl.when(b + 1 < pl.num_programs(0))
    def _():
        for cp in all_copies(b + 1, (b + 1) % 2):
            cp.start()

    slot = b % 2
    for cp in all_copies(b, slot):
        cp.wait()

    within = _scan_tokens(lbuf[slot], nh, page)
    lane = lax.broadcasted_iota(jnp.int32, (n_pages, LANE), 1)
    totals = jnp.where(lane >= LANE - nh, within[:, prow - LANE:], 0.0)
    pr = lax.broadcasted_iota(jnp.int32, (n_pages, n_pages), 0)
    pc = lax.broadcasted_iota(jnp.int32, (n_pages, n_pages), 1)
    before = _dot_sel_left((pr > pc).astype(bf16), totals)
    cflat = (within + jnp.concatenate([_spread_last_group(before, nh)] * (prow // LANE), axis=1)) * LOG2E
    whole = _spread_last_group(before[n_pages - 1:, :] + totals[n_pages - 1:, :], nh)
    c_new = (_scan_tokens(lfnew_ref[0], nh, LANE // nh) + whole) * LOG2E
    rr = lax.broadcasted_iota(jnp.int32, (rows_q, LANE), 0)
    ll = lax.broadcasted_iota(jnp.int32, (rows_q, LANE), 1)
    cq = jnp.sum(jnp.where(ll == rr, jnp.broadcast_to(c_new, (rows_q, LANE)), 0.0),
                 axis=-1, keepdims=True)
    q = q_ref[0]
    zero = jnp.zeros(q.shape, bf16)
    q2 = jnp.concatenate([jnp.concatenate([q, zero], axis=1), jnp.concatenate([zero, q], axis=1)], axis=0)

    def both(x):
        return jnp.concatenate([x, x], axis=0)

    m = jnp.full((rows_q, 1), -jnp.inf, f32)
    l = jnp.zeros((rows_q, 1), f32)
    acc = jnp.zeros((2 * rows_q, 2 * hd), f32)
    for blk in range(n_pages // ppc):
        row_a = blk * ppc * prow
        k2[blk, :, :hd] = kbuf[slot, row_a:row_a + half].astype(bf16)
        k2[blk, :, hd:] = kbuf[slot, row_a + half:row_a + 2 * half].astype(bf16)
        s = lax.dot_general(q2, k2[blk], _NT, preferred_element_type=f32)
        pg = blk * ppc
        bias = jnp.concatenate(
            [jnp.concatenate([jnp.broadcast_to(cflat[pg + p:pg + p + 1, :], (rows_q, prow)),
                              jnp.broadcast_to(cflat[pg + hp + p:pg + hp + p + 1, :], (rows_q, prow))], axis=0)
             for p in range(hp)], axis=1)
        r = lax.broadcasted_iota(jnp.int32, s.shape, 0)
        n = lax.broadcasted_iota(jnp.int32, s.shape, 1)
        t = jnp.where(n % nh == r % nh, s - bias, NEG)
        rowmax = jnp.max(t, axis=-1, keepdims=True)
        m_new = jnp.maximum(m, jnp.maximum(rowmax[:rows_q], rowmax[rows_q:]) + cq)
        alpha = jnp.exp2(m - m_new)
        p = jnp.exp2(t + both(cq - m_new))
        rowsum = jnp.sum(p, axis=-1, keepdims=True)
        l = alpha * l + rowsum[:rows_q] + rowsum[rows_q:]
        m = m_new
        v2[blk, :, :hd] = vbuf[slot, row_a:row_a + half].astype(bf16)
        v2[blk, :, hd:] = vbuf[slot, row_a + half:row_a + 2 * half].astype(bf16)
        acc = both(alpha) * acc + _dot(p.astype(bf16), v2[blk])

    knew_scr[:rows_q, :] = knew_ref[0]
    vnew_scr[:rows_q, :] = vnew_ref[0]
    s = lax.dot_general(q, knew_scr[...], _NT, preferred_element_type=f32)
    r = lax.broadcasted_iota(jnp.int32, s.shape, 0)
    n = lax.broadcasted_iota(jnp.int32, s.shape, 1)
    valid = jnp.logical_and(n % nh == r % nh, n // nh <= r // nh)
    t = jnp.where(valid, s - c_new, NEG)
    m_new = jnp.maximum(m, jnp.max(t, axis=-1, keepdims=True) + cq)
    alpha = jnp.exp2(m - m_new)
    p = jnp.exp2(t + (cq - m_new))
    l = alpha * l + jnp.sum(p, axis=-1, keepdims=True)
    acc = both(alpha) * acc
    o = acc[:rows_q, :hd] + acc[rows_q:, hd:] + _dot(p.astype(bf16), vnew_scr[...])
    o_ref[0] = o / l


def _fox_decode(page_table, q, knew, vnew, lfnew, cache_k, cache_v, cache_lf, *, nh, ppc):
    bs, rows_q, hd = q.shape
    ts = rows_q // nh
    n_pages = page_table.shape[1]
    prow = cache_k.shape[1]
    assert rows_q <= LANE and hd == LANE and prow % LANE == 0 and LANE % nh == 0
    assert ppc % 2 == 0 and n_pages % ppc == 0
    any_spec = pl.BlockSpec(memory_space=pl.ANY)
    per_b = lambda shape: pl.BlockSpec((1,) + shape, lambda s, pt: (s, 0, 0))
    half = ppc // 2 * prow
    grid_spec = pltpu.PrefetchScalarGridSpec(
        num_scalar_prefetch=1,
        grid=(bs,),
        in_specs=[per_b((rows_q, hd)), per_b((rows_q, hd)), per_b((rows_q, hd)), per_b((1, LANE)),
                  any_spec, any_spec, any_spec],
        out_specs=per_b((rows_q, hd)),
        scratch_shapes=[
            pltpu.VMEM((2, n_pages * prow, hd), f32), pltpu.VMEM((2, n_pages * prow, hd), f32),
            pltpu.VMEM((2, n_pages, prow), f32),
            pltpu.SemaphoreType.DMA((2,)), pltpu.SemaphoreType.DMA((2,)), pltpu.SemaphoreType.DMA((2,)),
            pltpu.VMEM((n_pages // ppc, half, 2 * hd), bf16), pltpu.VMEM((n_pages // ppc, half, 2 * hd), bf16),
            pltpu.VMEM((LANE, hd), bf16), pltpu.VMEM((LANE, hd), bf16),
        ],
    )
    kern = functools.partial(_fox_decode_kernel, nh=nh, hd=hd, ts=ts, n_pages=n_pages, ppc=ppc)
    return pl.pallas_call(
        kern,
        grid_spec=grid_spec,
        out_shape=jax.ShapeDtypeStruct((bs, rows_q, hd), f32),
        compiler_params=_params(("arbitrary",), 56),
        name="fox_decode",
    )(page_table, q, knew, vnew, lfnew, cache_k, cache_v, cache_lf)


def _tile(n, pref):
    while n % pref:
        pref //= 2
    return pref


def kernel(x_prompt, x_sample, cache_k, cache_v, cache_logf, state_gla, page_table, p_prompt, p_sample,
           g_mix, w_in, b_f, w_gk2, b_gk, g_gla_out, w_out, g_mlp, w_up, w_down, w_ple, g_ple, g_ple_gate,
           w_ple_gate, g_final):
    batch, seq, d = x_prompt.shape
    bs, ts, _ = x_sample.shape
    depth = g_mix.shape[0]
    _, n_pool, page, nh, hd = cache_k.shape
    _, _, nh_b, dk, dv = state_gla.shape
    w_a, wk_b, wv_b = nh * hd, nh_b * dk, nh_b * dv
    rank = w_gk2.shape[1]
    n_p, n_s = batch * seq, bs * ts
    assert ts <= SUBLANE and ts % GLA_CHUNK != 0 and seq % GLA_CHUNK == 0
    row = lambda a: a.reshape(1, -1)

    xp = x_prompt.reshape(n_p, d)
    xs = x_sample.reshape(n_s, d)
    outs = {k: [] for k in ("kp", "vp", "fp", "sp", "ks", "vs", "fs", "ss")}
    for l in range(depth):
        final = l == depth - 1
        wl = w_in[l]
        o_f = 3 * w_a
        o_b = o_f + nh
        o_g = o_b + 2 * wk_b + 2 * wv_b
        w_qkv, w_gla = wl[:, :o_f].astype(bf16), wl[:, o_b:o_g].astype(bf16)
        w_fg = jnp.concatenate([wl[:, o_f:o_b], wl[:, o_g:], jnp.zeros((d, LANE - nh - rank), f32)], axis=1).astype(bf16)
        bf_pad = jnp.zeros((1, LANE), f32).at[0, :nh].set(b_f[l])
        wgk_pad = jnp.zeros((LANE, wk_b), f32).at[nh:nh + rank].set(w_gk2[l]).astype(bf16)
        w_out_b, w_ple_b, w_gate_b = w_out[l].astype(bf16), w_ple[l].astype(bf16), w_ple_gate[l].astype(bf16)
        inproj = functools.partial(_inproj, g=row(g_mix[l]), w_qkv=w_qkv, w_gla=w_gla, w_fg=w_fg, nh=nh, hd=hd,
                                   wk_b=wk_b, wv_b=wv_b, scale_a=hd ** -0.5 * LOG2E, scale_b=dk ** -0.5)
        gla = functools.partial(_gla, wgk_pad=wgk_pad, bgk=row(b_gk[l]), gout=row(g_gla_out[l]))

        def tail(x, oa, ob, p, n, mlp_weights):
            x = _outproj(oa, ob, x, w_out_b, tm=_tile(n, 512))
            if mlp_weights[0].dtype == bf16:
                x = _ffn(x, row(g_mlp[l]), *mlp_weights, tm=_tile(n, 512), tf=1024)
            else:
                x, *mlp_weights = _ffn(x, row(g_mlp[l]), *mlp_weights, tm=n, tf=512)
            x = _ple(x, p.reshape(n, -1), w_ple_b, row(g_ple[l]), row(g_ple_gate[l]), w_gate_b,
                     row(g_final), tm=_tile(n, 512), final=final)
            return x, mlp_weights

        qa, kab, vab, qb, kb, vb, rb, ka, va, fg = inproj(xs, tm=_tile(n_s, 256))
        logf, _, _ = _fox_prep(fg, bf_pad, batch=1, nh=nh)
        key_rows = lambda a: a.reshape(bs, ts * nh, hd)
        lfnew = jnp.pad(logf.reshape(bs, 1, ts * nh), ((0, 0), (0, 0), (0, LANE - ts * nh)))
        oa = _fox_decode(page_table, key_rows(qa), key_rows(kab), key_rows(vab), lfnew,
                         cache_k[l].reshape(n_pool, page * nh, hd), cache_v[l].reshape(n_pool, page * nh, hd),
                         cache_logf[l].reshape(n_pool, 1, page * nh), nh=nh, ppc=4)
        oa = oa.reshape(n_s, w_a).astype(bf16)
        tpad = 16
        flat = lambda a: jnp.pad(a.reshape(bs, ts, -1), ((0, 0), (0, tpad - ts), (0, 0))).reshape(bs * tpad, -1)
        ob, s_s = gla(flat(qb), flat(kb), flat(vb), flat(rb), flat(fg), s0=state_gla[l], batch=bs,
                      tc=tpad, n_seq=_tile(bs, 8), chunk=tpad, t_valid=ts)
        ob = ob.reshape(bs, tpad, wv_b)[:, :ts].reshape(n_s, wv_b)
        xs, mlp_weights = tail(xs, oa, ob, p_sample[l], n_s, (w_up[l], w_down[l]))
        outs["ks"].append(ka.reshape(bs, ts, nh, hd))
        outs["vs"].append(va.reshape(bs, ts, nh, hd))
        outs["fs"].append(logf.reshape(bs, ts, nh))
        outs["ss"].append(s_s)

        qa, kab, vab, qb, kb, vb, rb, ka, va, fg = inproj(xp, tm=_tile(n_p, 256))
        logf, c, ct = _fox_prep(fg, bf_pad, batch=batch, nh=nh)
        oa = _fox_attn(qa, kab, vab, c, ct, batch=batch, nh=nh, hd=hd, tq=_tile(seq, 256))
        ob, s_p = gla(qb, kb, vb, rb, fg, s0=jnp.zeros((batch, nh_b, dk, dv), f32), batch=batch,
                      tc=_tile(seq, 1024), n_seq=1, chunk=GLA_CHUNK, t_valid=None)
        xp, _ = tail(xp, oa, ob, p_prompt[l], n_p, mlp_weights)
        outs["kp"].append(ka.reshape(batch, seq, nh, hd))
        outs["vp"].append(va.reshape(batch, seq, nh, hd))
        outs["fp"].append(logf.reshape(batch, seq, nh))
        outs["sp"].append(s_p)

    st = lambda key: jnp.stack(outs[key])
    return (xp.reshape(batch, seq, d), xs.reshape(bs, ts, d), st("kp"), st("vp"), st("fp"), st("sp"),
            st("ks"), st("vs"), st("fs"), st("ss"))
```

```python
import functools

import jax
import jax.numpy as jnp
from jax import lax
from jax.experimental import pallas as pl
from jax.experimental.pallas import tpu as pltpu

f32 = jnp.float32
bf16 = jnp.bfloat16

EPS = 1e-6
NEG = -1e30
GK_NORM = 16.0
GLA_CHUNK = 64
LOG2E = 1.4426950408889634

LANE = 128
SUBLANE = 8
MIB = 2**20

_NT = (((1,), (1,)), ((), ()))
_TN = (((0,), (0,)), ((), ()))


def _params(semantics, vmem_mib):
    return pltpu.CompilerParams(dimension_semantics=semantics, vmem_limit_bytes=vmem_mib * MIB)


def _resident(shape):
    return pl.BlockSpec(shape, lambda *_: (0,) * len(shape), pipeline_mode=pl.Buffered(1))


def _rms(x, g):
    return x * lax.rsqrt(jnp.mean(x * x, axis=-1, keepdims=True) + EPS) * g


def _log_sigmoid(x):
    return jnp.minimum(x, 0.0) - jnp.log1p(jnp.exp(-jnp.abs(x)))


def _split3(x):
    hi = x.astype(bf16)
    r = x - hi.astype(f32)
    mid = r.astype(bf16)
    lo = (r - mid.astype(f32)).astype(bf16)
    return hi, mid, lo


def _dot(a, b):
    return jnp.dot(a, b, preferred_element_type=f32)


def _dot_sel_left(sel, x):
    return sum(_dot(sel, t) for t in _split3(x))


def _tri(n):
    r = lax.broadcasted_iota(jnp.int32, (n, n), 0)
    c = lax.broadcasted_iota(jnp.int32, (n, n), 1)
    return (r >= c).astype(bf16)


def _repack_kernel(w_ref, qkv_ref, gla_ref, fg_ref, *, o_f, o_b, o_g):
    rows, cols = w_ref.shape
    qkv_ref[...] = w_ref[:, :o_f].astype(bf16)
    gla_ref[...] = w_ref[:, o_b:o_g].astype(bf16)
    pad = jnp.zeros((rows, LANE - (o_b - o_f) - (cols - o_g)), f32)
    fg_ref[...] = jnp.concatenate([w_ref[:, o_f:o_b], w_ref[:, o_g:], pad], axis=1).astype(bf16)


def _repack_w_in(w, *, o_f, o_b, o_g, tm):
    d, cols = w.shape
    row = lambda width: pl.BlockSpec((tm, width), lambda i: (i, 0))
    sds = jax.ShapeDtypeStruct
    return pl.pallas_call(
        functools.partial(_repack_kernel, o_f=o_f, o_b=o_b, o_g=o_g),
        grid=(d // tm,),
        in_specs=[row(cols)],
        out_specs=[row(o_f), row(o_g - o_b), row(LANE)],
        out_shape=[sds((d, o_f), bf16), sds((d, o_g - o_b), bf16), sds((d, LANE), bf16)],
        compiler_params=_params(("parallel",), 40),
        name="repack_w_in",
    )(w)


def _inproj_kernel(x_ref, g_ref, wa_ref, wb_ref, wfg_ref,
                   qa_ref, kab_ref, vab_ref, qb_ref, kb_ref, vb_ref, rb_ref, ka_ref, va_ref, fg_ref,
                   *, w_a, wk_b, wv_b, scale_a, scale_b, chunk):
    h = _rms(x_ref[...], g_ref[...]).astype(bf16)

    def section(w_ref, col, width, store):
        for j in range(0, width, chunk):
            store(j, _dot(h, w_ref[:, col + j:col + j + chunk]))

    def st_qa(j, z):
        qa_ref[:, j:j + chunk] = (z * scale_a).astype(bf16)

    def st_ka(j, z):
        ka_ref[:, j:j + chunk] = z
        kab_ref[:, j:j + chunk] = z.astype(bf16)

    def st_va(j, z):
        va_ref[:, j:j + chunk] = z
        vab_ref[:, j:j + chunk] = z.astype(bf16)

    def st_qb(j, z):
        qb_ref[:, j:j + chunk] = (z * scale_b).astype(bf16)

    def st_kb(j, z):
        kb_ref[:, j:j + chunk] = z.astype(bf16)

    def st_vb(j, z):
        vb_ref[:, j:j + chunk] = z.astype(bf16)

    def st_rb(j, z):
        rb_ref[:, j:j + chunk] = z.astype(bf16)

    section(wa_ref, 0, w_a, st_qa)
    section(wa_ref, w_a, w_a, st_ka)
    section(wa_ref, 2 * w_a, w_a, st_va)
    section(wb_ref, 0, wk_b, st_qb)
    section(wb_ref, wk_b, wk_b, st_kb)
    section(wb_ref, 2 * wk_b, wv_b, st_vb)
    section(wb_ref, 2 * wk_b + wv_b, wv_b, st_rb)
    fg_ref[...] = _dot(h, wfg_ref[...])


def _inproj(x, g, w_qkv, w_gla, w_fg, *, nh, hd, wk_b, wv_b, scale_a, scale_b, tm):
    n, d = x.shape
    w_a = nh * hd
    row = lambda width: pl.BlockSpec((tm, width), lambda i: (i, 0))
    sds = jax.ShapeDtypeStruct
    kern = functools.partial(_inproj_kernel, w_a=w_a, wk_b=wk_b, wv_b=wv_b,
                             scale_a=scale_a, scale_b=scale_b, chunk=512)
    return pl.pallas_call(
        kern,
        grid=(n // tm,),
        in_specs=[row(d), _resident((1, d)), _resident(w_qkv.shape), _resident(w_gla.shape), _resident((d, LANE))],
        out_specs=[row(w_a), row(w_a), row(w_a), row(wk_b), row(wk_b), row(wv_b), row(wv_b),
                   row(w_a), row(w_a), row(LANE)],
        out_shape=[sds((n, w_a), bf16), sds((n, w_a), bf16), sds((n, w_a), bf16),
                   sds((n, wk_b), bf16), sds((n, wk_b), bf16), sds((n, wv_b), bf16), sds((n, wv_b), bf16),
                   sds((n, w_a), f32), sds((n, w_a), f32), sds((n, LANE), f32)],
        compiler_params=_params(("parallel",), 52),
        name="inproj",
    )(x, g, w_qkv, w_gla, w_fg)


def _fox_prep_kernel(fg_ref, bf_ref, logf_ref, c_ref, ct_ref, *, nh, blk):
    t = fg_ref.shape[0]
    tri = _tri(blk)
    carry = jnp.zeros((1, LANE), f32)
    for b in range(t // blk):
        rows = slice(b * blk, (b + 1) * blk)
        logf = _log_sigmoid(fg_ref[rows, :] + bf_ref[...])
        logf_ref[rows, :] = logf[:, :nh]
        c = _dot_sel_left(tri, logf) + carry
        carry = c[blk - 1:blk, :]
        c2 = c * LOG2E
        c_ref[rows, :] = c2
        ct_ref[0, :, rows] = c2.T[:nh, :]


def _fox_prep(fg, bf_pad, *, batch, nh):
    n = fg.shape[0]
    t = n // batch
    blk = min(t, 256)
    sds = jax.ShapeDtypeStruct
    return pl.pallas_call(
        functools.partial(_fox_prep_kernel, nh=nh, blk=blk),
        grid=(batch,),
        in_specs=[pl.BlockSpec((t, LANE), lambda b: (b, 0)), _resident((1, LANE))],
        out_specs=[pl.BlockSpec((t, nh), lambda b: (b, 0)),
                   pl.BlockSpec((t, LANE), lambda b: (b, 0)),
                   pl.BlockSpec((1, nh, t), lambda b: (b, 0, 0))],
        out_shape=[sds((n, nh), f32), sds((n, LANE), f32), sds((batch, nh, t), f32)],
        compiler_params=_params(("parallel",), 32),
        name="fox_prep",
    )(fg, bf_pad)


def _fox_attn_kernel(q_ref, k_ref, v_ref, c_ref, ct_ref, o_ref, m_scr, l_scr, acc_scr, cq_scr, *, nh, hd, tq):
    i = pl.program_id(1)
    row = lax.broadcasted_iota(jnp.int32, (tq, tq), 0)
    col = lax.broadcasted_iota(jnp.int32, (tq, tq), 1)
    causal = col <= row
    m_scr[...] = jnp.full(m_scr.shape, -jnp.inf, f32)
    l_scr[...] = jnp.zeros(l_scr.shape, f32)
    acc_scr[...] = jnp.zeros(acc_scr.shape, f32)
    for h in range(nh):
        cq_scr[h] = jnp.broadcast_to(c_ref[:, h:h + 1], (tq, LANE))

    def weights(h, t):
        m, cq = m_scr[h], cq_scr[h]
        m_new = jnp.maximum(m, jnp.max(t, axis=-1, keepdims=True) + cq)
        alpha = jnp.exp2(m - m_new)
        p = jnp.exp2(t + jnp.concatenate([cq - m_new] * (t.shape[1] // LANE), axis=1))
        l_scr[h] = alpha * l_scr[h] + jnp.sum(p, axis=-1, keepdims=True)
        m_scr[h] = m_new
        return p.astype(bf16), alpha

    def block(start, width, diag):
        keys = pl.ds(pl.multiple_of(start, width), width)
        for h0 in range(0, nh, 2):
            ps, alphas = [], []
            for h in (h0, h0 + 1):
                lanes = slice(h * hd, (h + 1) * hd)
                t = lax.dot_general(q_ref[:, lanes], k_ref[keys, lanes], _NT, preferred_element_type=f32)
                t = t - ct_ref[0, h:h + 1, keys]
                if diag:
                    t = jnp.where(causal, t, NEG)
                p, alpha = weights(h, t)
                ps.append(p)
                alphas.append(alpha)
            pv = _dot(jnp.concatenate(ps, axis=0), v_ref[keys, h0 * hd:(h0 + 2) * hd])
            acc_scr[h0] = alphas[0] * acc_scr[h0] + pv[:tq, :hd]
            acc_scr[h0 + 1] = alphas[1] * acc_scr[h0 + 1] + pv[tq:, hd:]

    def off_diagonal_pair(j, carry):
        block(j * 2 * tq, 2 * tq, False)
        return carry

    lax.fori_loop(0, i // 2, off_diagonal_pair, 0)

    @pl.when(i % 2 == 1)
    def _():
        block((i - 1) * tq, tq, False)

    block(i * tq, tq, True)
    for h in range(nh):
        o_ref[:, h * hd:(h + 1) * hd] = (acc_scr[h] / l_scr[h]).astype(bf16)


def _fox_attn(qa, kab, vab, c, ct, *, batch, nh, hd, tq):
    n, w_a = qa.shape
    t = n // batch
    nq = t // tq
    assert hd == LANE and nh % 2 == 0 and tq % LANE == 0
    return pl.pallas_call(
        functools.partial(_fox_attn_kernel, nh=nh, hd=hd, tq=tq),
        grid=(batch, nq),
        in_specs=[pl.BlockSpec((tq, w_a), lambda b, i: (b * nq + i, 0)),
                  pl.BlockSpec((t, w_a), lambda b, i: (b, 0)),
                  pl.BlockSpec((t, w_a), lambda b, i: (b, 0)),
                  pl.BlockSpec((tq, LANE), lambda b, i: (b * nq + i, 0)),
                  pl.BlockSpec((1, nh, t), lambda b, i: (b, 0, 0))],
        out_specs=pl.BlockSpec((tq, w_a), lambda b, i: (b * nq + i, 0)),
        out_shape=jax.ShapeDtypeStruct((n, w_a), bf16),
        scratch_shapes=[pltpu.VMEM((nh, tq, LANE), f32)] * 4,
        compiler_params=_params(("parallel", "arbitrary"), 40),
        name="fox_attn",
    )(qa, kab, vab, c, ct)


def _gla_kernel(q_ref, k_ref, v_ref, r_ref, fg_ref, wgk_ref, bgk_ref, gout_ref, s0_ref,
                o_ref, st_ref, s_scr, *, nh, dk, dv, chunk, n_chunks, n_seq, t_valid):
    t = pl.program_id(1)

    @pl.when(t == 0)
    def _():
        s_scr[...] = s0_ref[...]

    tri = _tri(chunk)
    row = lax.broadcasted_iota(jnp.int32, (chunk, chunk), 0)
    col = lax.broadcasted_iota(jnp.int32, (chunk, chunk), 1)
    causal = col <= row
    items = [(e, ci) for e in range(n_seq) for ci in range(n_chunks)]
    rows_of = lambda it: slice((it[0] * n_chunks + it[1]) * chunk, (it[0] * n_chunks + it[1] + 1) * chunk)
    heads = [(slice(h * dk, (h + 1) * dk), slice(h * dv, (h + 1) * dv)) for h in range(nh)]

    def stage1(it):
        la = _log_sigmoid(_dot(fg_ref[rows_of(it), :].astype(bf16), wgk_ref[...]) + bgk_ref[...]) / GK_NORM
        if t_valid is not None:
            la = jnp.where(lax.broadcasted_iota(jnp.int32, la.shape, 0) < t_valid, la, 0.0)
        return la

    def stage2(la):
        b = _dot_sel_left(tri, la)
        b_last = b[chunk - 1:chunk, :]
        return jnp.exp(b), jnp.exp(-b), jnp.exp(b_last - b), jnp.exp(b_last)

    def stage3(it, exps):
        e_pos, e_neg, e_rem, e_last = exps
        rows = rows_of(it)
        out = []
        for kl, _ in heads:
            q = q_ref[rows, kl].astype(f32)
            k = k_ref[rows, kl].astype(f32)
            qe = (q * e_pos[:, kl]).astype(bf16)
            ke = (k * e_neg[:, kl]).astype(bf16)
            kd = (k * e_rem[:, kl]).astype(bf16)
            a = lax.dot_general(qe, ke, _NT, preferred_element_type=f32)
            a = jnp.where(causal, a, 0.0).astype(bf16)
            decay = jnp.broadcast_to(e_last[:, kl], (SUBLANE, dk)).T[:, :1]
            out.append((qe, kd, a, decay))
        return out

    def stage4(it, per_head):
        rows = rows_of(it)
        for h, ((_, vl), (qe, kd, a, decay)) in enumerate(zip(heads, per_head)):
            v = v_ref[rows, vl]
            s = s_scr[it[0], h]
            o = _dot(qe, s.astype(bf16)) + _dot(a, v)
            s_scr[it[0], h] = s * decay + lax.dot_general(kd, v, _TN, preferred_element_type=f32)
            gated = _rms(o, gout_ref[...]) * jax.nn.silu(r_ref[rows, vl].astype(f32))
            o_ref[rows, vl] = gated.astype(bf16)

    r1, r2, r3 = {}, {}, {}
    for tick in range(len(items) + 3):
        if 0 <= tick - 3 < len(items):
            stage4(items[tick - 3], r3.pop(tick - 3))
        if 0 <= tick - 2 < len(items):
            r3[tick - 2] = stage3(items[tick - 2], r2.pop(tick - 2))
        if 0 <= tick - 1 < len(items):
            r2[tick - 1] = stage2(r1.pop(tick - 1))
        if tick < len(items):
            r1[tick] = stage1(items[tick])

    @pl.when(t == pl.num_programs(1) - 1)
    def _():
        st_ref[...] = s_scr[...]


def _gla(qb, kb, vb, rb, fg, wgk_pad, bgk, gout, s0, *, batch, tc, n_seq, chunk, t_valid):
    n, wk_b = qb.shape
    wv_b = vb.shape[1]
    _, nh, dk, dv = s0.shape
    nt = n // batch // tc
    assert n_seq == 1 or nt == 1
    rowspec = lambda width: pl.BlockSpec((n_seq * tc, width), lambda b, t: (b * nt + t, 0))
    state = pl.BlockSpec((n_seq, nh, dk, dv), lambda b, t: (b, 0, 0, 0))
    kern = functools.partial(_gla_kernel, nh=nh, dk=dk, dv=dv, chunk=chunk, n_chunks=tc // chunk, n_seq=n_seq,
                             t_valid=t_valid)
    return pl.pallas_call(
        kern,
        grid=(batch // n_seq, nt),
        in_specs=[rowspec(wk_b), rowspec(wk_b), rowspec(wv_b), rowspec(wv_b), rowspec(LANE),
                  _resident((LANE, wk_b)), _resident((1, wk_b)), _resident((1, dv)), state],
        out_specs=[rowspec(wv_b), state],
        out_shape=[jax.ShapeDtypeStruct((n, wv_b), bf16), jax.ShapeDtypeStruct(s0.shape, f32)],
        scratch_shapes=[pltpu.VMEM((n_seq, nh, dk, dv), f32)],
        compiler_params=_params(("parallel", "arbitrary"), 32),
        name="gla",
    )(qb, kb, vb, rb, fg, wgk_pad, bgk, gout, s0)


def _outproj_kernel(oa_ref, ob_ref, x_ref, w_ref, x1_ref, *, w_a):
    mixed = _dot(oa_ref[...], w_ref[:w_a, :]) + _dot(ob_ref[...], w_ref[w_a:, :])
    x1_ref[...] = x_ref[...] + mixed


def _outproj(oa, ob, x, w_out, *, tm):
    n, d = x.shape
    w_a, wv_b = oa.shape[1], ob.shape[1]
    row = lambda width: pl.BlockSpec((tm, width), lambda i: (i, 0))
    return pl.pallas_call(
        functools.partial(_outproj_kernel, w_a=w_a),
        grid=(n // tm,),
        in_specs=[row(w_a), row(wv_b), row(d), _resident((w_a + wv_b, d))],
        out_specs=row(d),
        out_shape=jax.ShapeDtypeStruct((n, d), f32),
        compiler_params=_params(("parallel",), 40),
        name="outproj",
    )(oa, ob, x, w_out)


def _ffn_kernel(x_ref, g_ref, wup_ref, wdn_ref, o_ref, *rest):
    h_scr = rest[-1]

    @pl.when(pl.program_id(1) == 0)
    def _():
        x = x_ref[...]
        h_scr[...] = _rms(x, g_ref[...]).astype(bf16)
        o_ref[...] = x

    w_up, w_down = wup_ref[...].astype(bf16), wdn_ref[...].astype(bf16)
    if len(rest) == 3:
        rest[0][...] = w_up
        rest[1][...] = w_down
    a = _dot(h_scr[...], w_up)
    a = jnp.square(jnp.maximum(a, 0.0)).astype(bf16)
    o_ref[...] += _dot(a, w_down)


def _ffn(x, g, w_up, w_down, *, tm, tf):
    n, d = x.shape
    dff = w_up.shape[1]
    emit_cast = w_up.dtype != bf16
    assert not emit_cast or n == tm
    x_spec = pl.BlockSpec((tm, d), lambda i, f: (i, 0))
    up_spec = pl.BlockSpec((d, tf), lambda i, f: (0, f))
    down_spec = pl.BlockSpec((tf, d), lambda i, f: (f, 0))
    sds = jax.ShapeDtypeStruct
    out = pl.pallas_call(
        _ffn_kernel,
        grid=(n // tm, dff // tf),
        in_specs=[x_spec, _resident((1, d)), up_spec, down_spec],
        out_specs=[x_spec] + ([up_spec, down_spec] if emit_cast else []),
        out_shape=[sds((n, d), f32)] + ([sds(w_up.shape, bf16), sds(w_down.shape, bf16)] if emit_cast else []),
        scratch_shapes=[pltpu.VMEM((tm, d), bf16)],
        compiler_params=_params(("parallel", "arbitrary"), 52),
        name="ffn",
    )(x, g, w_up, w_down)
    return out if emit_cast else out[0]


def _ple_kernel(x_ref, p_ref, wple_ref, gple_ref, ggate_ref, wgate_ref, gfin_ref, y_ref, *, final):
    x = x_ref[...]
    e = _rms(_dot(p_ref[...].astype(bf16), wple_ref[...]), gple_ref[...])
    gate = jax.nn.sigmoid(_dot(_rms(x, ggate_ref[...]).astype(bf16), wgate_ref[...]))
    x = x + gate * e
    y_ref[...] = _rms(x, gfin_ref[...]) if final else x


def _ple(x, p, w_ple, g_ple, g_gate, w_gate, g_final, *, tm, final):
    n, d = x.shape
    pd = p.shape[1]
    row = lambda width: pl.BlockSpec((tm, width), lambda i: (i, 0))
    return pl.pallas_call(
        functools.partial(_ple_kernel, final=final),
        grid=(n // tm,),
        in_specs=[row(d), row(pd), _resident((pd, d)), _resident((1, d)), _resident((1, d)),
                  _resident((d, d)), _resident((1, d))],
        out_specs=row(d),
        out_shape=jax.ShapeDtypeStruct((n, d), f32),
        compiler_params=_params(("parallel",), 40),
        name="ple",
    )(x, p, w_ple, g_ple, g_gate, w_gate, g_final)


def _scan_tokens(x, nh, n_tokens):
    lane = lax.broadcasted_iota(jnp.int32, x.shape, 1)
    shift = nh
    while shift < n_tokens * nh:
        x = x + jnp.where(lane >= shift, pltpu.roll(x, shift, axis=1), 0.0)
        shift *= 2
    return x


def _spread_last_group(x, nh):
    shift = nh
    while shift < LANE:
        x = x + pltpu.roll(x, shift, axis=1)
        shift *= 2
    return x


def _fox_decode_kernel(pt_ref, q_ref, knew_ref, vnew_ref, lfnew_ref, ck_hbm, cv_hbm, clf_hbm, o_ref,
                       kbuf, vbuf, lbuf, ksem, vsem, lsem, k2, v2, knew_scr, vnew_scr,
                       *, nh, hd, ts, n_pages, ppc):
    b = pl.program_id(0)
    prow = lbuf.shape[2]
    page = prow // nh
    hp = ppc // 2
    half = hp * prow
    rows_q = ts * nh

    def page_copies(hbm, buf, sem, rows, bb, slot):
        return [pltpu.make_async_copy(hbm.at[pt_ref[bb, p]], buf.at[slot, pl.ds(p * rows, rows)], sem.at[slot])
                for p in range(n_pages)]

    def all_copies(bb, slot):
        return (page_copies(ck_hbm, kbuf, ksem, prow, bb, slot) + page_copies(cv_hbm, vbuf, vsem, prow, bb, slot)
                + page_copies(clf_hbm, lbuf, lsem, 1, bb, slot))

    @pl.when(b == 0)
    def _():
        for cp in all_copies(0, 0):
            cp.start()
        knew_scr[...] = jnp.zeros(knew_scr.shape, bf16)
        vnew_scr[...] = jnp.zeros(vnew_scr.shape, bf16)

    @pl.when(b + 1 < pl.num_programs(0))
    def _():
        for cp in all_copies(b + 1, (b + 1) % 2):
            cp.start()

    slot = b % 2
    for cp in all_copies(b, slot):
        cp.wait()

    within = _scan_tokens(lbuf[slot], nh, page)
    lane = lax.broadcasted_iota(jnp.int32, (n_pages, LANE), 1)
    totals = jnp.where(lane >= LANE - nh, within[:, prow - LANE:], 0.0)
    pr = lax.broadcasted_iota(jnp.int32, (n_pages, n_pages), 0)
    pc = lax.broadcasted_iota(jnp.int32, (n_pages, n_pages), 1)
    before = _dot_sel_left((pr > pc).astype(bf16), totals)
    cflat = (within + jnp.concatenate([_spread_last_group(before, nh)] * (prow // LANE), axis=1)) * LOG2E
    whole = _spread_last_group(before[n_pages - 1:, :] + totals[n_pages - 1:, :], nh)
    c_new = (_scan_tokens(lfnew_ref[0], nh, LANE // nh) + whole) * LOG2E
    rr = lax.broadcasted_iota(jnp.int32, (rows_q, LANE), 0)
    ll = lax.broadcasted_iota(jnp.int32, (rows_q, LANE), 1)
    cq = jnp.sum(jnp.where(ll == rr, jnp.broadcast_to(c_new, (rows_q, LANE)), 0.0),
                 axis=-1, keepdims=True)
    q = q_ref[0]
    zero = jnp.zeros(q.shape, bf16)
    q2 = jnp.concatenate([jnp.concatenate([q, zero], axis=1), jnp.concatenate([zero, q], axis=1)], axis=0)

    def both(x):
        return jnp.concatenate([x, x], axis=0)

    m = jnp.full((rows_q, 1), -jnp.inf, f32)
    l = jnp.zeros((rows_q, 1), f32)
    acc = jnp.zeros((2 * rows_q, 2 * hd), f32)
    for blk in range(n_pages // ppc):
        row_a = blk * ppc * prow
        k2[blk, :, :hd] = kbuf[slot, row_a:row_a + half].astype(bf16)
        k2[blk, :, hd:] = kbuf[slot, row_a + half:row_a + 2 * half].astype(bf16)
        s = lax.dot_general(q2, k2[blk], _NT, preferred_element_type=f32)
        pg = blk * ppc
        bias = jnp.concatenate(
            [jnp.concatenate([jnp.broadcast_to(cflat[pg + p:pg + p + 1, :], (rows_q, prow)),
                              jnp.broadcast_to(cflat[pg + hp + p:pg + hp + p + 1, :], (rows_q, prow))], axis=0)
             for p in range(hp)], axis=1)
        r = lax.broadcasted_iota(jnp.int32, s.shape, 0)
        n = lax.broadcasted_iota(jnp.int32, s.shape, 1)
        t = jnp.where(n % nh == r % nh, s - bias, NEG)
        rowmax = jnp.max(t, axis=-1, keepdims=True)
        m_new = jnp.maximum(m, jnp.maximum(rowmax[:rows_q], rowmax[rows_q:]) + cq)
        alpha = jnp.exp2(m - m_new)
        p = jnp.exp2(t + both(cq - m_new))
        rowsum = jnp.sum(p, axis=-1, keepdims=True)
        l = alpha * l + rowsum[:rows_q] + rowsum[rows_q:]
        m = m_new
        v2[blk, :, :hd] = vbuf[slot, row_a:row_a + half].astype(bf16)
        v2[blk, :, hd:] = vbuf[slot, row_a + half:row_a + 2 * half].astype(bf16)
        acc = both(alpha) * acc + _dot(p.astype(bf16), v2[blk])

    knew_scr[:rows_q, :] = knew_ref[0]
    vnew_scr[:rows_q, :] = vnew_ref[0]
    s = lax.dot_general(q, knew_scr[...], _NT, preferred_element_type=f32)
    r = lax.broadcasted_iota(jnp.int32, s.shape, 0)
    n = lax.broadcasted_iota(jnp.int32, s.shape, 1)
    valid = jnp.logical_and(n % nh == r % nh, n // nh <= r // nh)
    t = jnp.where(valid, s - c_new, NEG)
    m_new = jnp.maximum(m, jnp.max(t, axis=-1, keepdims=True) + cq)
    alpha = jnp.exp2(m - m_new)
    p = jnp.exp2(t + (cq - m_new))
    l = alpha * l + jnp.sum(p, axis=-1, keepdims=True)
    acc = both(alpha) * acc
    o = acc[:rows_q, :hd] + acc[rows_q:, hd:] + _dot(p.astype(bf16), vnew_scr[...])
    o_ref[0] = o / l


def _fox_decode(page_table, q, knew, vnew, lfnew, cache_k, cache_v, cache_lf, *, nh, ppc):
    bs, rows_q, hd = q.shape
    ts = rows_q // nh
    n_pages = page_table.shape[1]
    prow = cache_k.shape[1]
    assert rows_q <= LANE and hd == LANE and prow % LANE == 0 and LANE % nh == 0
    assert ppc % 2 == 0 and n_pages % ppc == 0
    any_spec = pl.BlockSpec(memory_space=pl.ANY)
    per_b = lambda shape: pl.BlockSpec((1,) + shape, lambda s, pt: (s, 0, 0))
    half = ppc // 2 * prow
    grid_spec = pltpu.PrefetchScalarGridSpec(
        num_scalar_prefetch=1,
        grid=(bs,),
        in_specs=[per_b((rows_q, hd)), per_b((rows_q, hd)), per_b((rows_q, hd)), per_b((1, LANE)),
                  any_spec, any_spec, any_spec],
        out_specs=per_b((rows_q, hd)),
        scratch_shapes=[
            pltpu.VMEM((2, n_pages * prow, hd), f32), pltpu.VMEM((2, n_pages * prow, hd), f32),
            pltpu.VMEM((2, n_pages, prow), f32),
            pltpu.SemaphoreType.DMA((2,)), pltpu.SemaphoreType.DMA((2,)), pltpu.SemaphoreType.DMA((2,)),
            pltpu.VMEM((n_pages // ppc, half, 2 * hd), bf16), pltpu.VMEM((n_pages // ppc, half, 2 * hd), bf16),
            pltpu.VMEM((LANE, hd), bf16), pltpu.VMEM((LANE, hd), bf16),
        ],
    )
    kern = functools.partial(_fox_decode_kernel, nh=nh, hd=hd, ts=ts, n_pages=n_pages, ppc=ppc)
    return pl.pallas_call(
        kern,
        grid_spec=grid_spec,
        out_shape=jax.ShapeDtypeStruct((bs, rows_q, hd), f32),
        compiler_params=_params(("arbitrary",), 56),
        name="fox_decode",
    )(page_table, q, knew, vnew, lfnew, cache_k, cache_v, cache_lf)


def _tile(n, pref):
    while n % pref:
        pref //= 2
    return pref


def kernel(x_prompt, x_sample, cache_k, cache_v, cache_logf, state_gla, page_table, p_prompt, p_sample,
           g_mix, w_in, b_f, w_gk2, b_gk, g_gla_out, w_out, g_mlp, w_up, w_down, w_ple, g_ple, g_ple_gate,
           w_ple_gate, g_final):
    batch, seq, d = x_prompt.shape
    bs, ts, _ = x_sample.shape
    depth = g_mix.shape[0]
    _, n_pool, page, nh, hd = cache_k.shape
    _, _, nh_b, dk, dv = state_gla.shape
    w_a, wk_b, wv_b = nh * hd, nh_b * dk, nh_b * dv
    rank = w_gk2.shape[1]
    n_p, n_s = batch * seq, bs * ts
    assert ts <= SUBLANE and ts % GLA_CHUNK != 0 and seq % GLA_CHUNK == 0
    row = lambda a: a.reshape(1, -1)

    xp = x_prompt.reshape(n_p, d)
    xs = x_sample.reshape(n_s, d)
    outs = {k: [] for k in ("kp", "vp", "fp", "sp", "ks", "vs", "fs", "ss")}
    for l in range(depth):
        final = l == depth - 1
        o_f = 3 * w_a
        o_b = o_f + nh
        o_g = o_b + 2 * wk_b + 2 * wv_b
        w_qkv, w_gla, w_fg = _repack_w_in(w_in[l], o_f=o_f, o_b=o_b, o_g=o_g, tm=_tile(d, 256))
        bf_pad = jnp.zeros((1, LANE), f32).at[0, :nh].set(b_f[l])
        wgk_pad = jnp.zeros((LANE, wk_b), f32).at[nh:nh + rank].set(w_gk2[l]).astype(bf16)
        w_out_b, w_ple_b, w_gate_b = w_out[l].astype(bf16), w_ple[l].astype(bf16), w_ple_gate[l].astype(bf16)
        inproj = functools.partial(_inproj, g=row(g_mix[l]), w_qkv=w_qkv, w_gla=w_gla, w_fg=w_fg, nh=nh, hd=hd,
                                   wk_b=wk_b, wv_b=wv_b, scale_a=hd ** -0.5 * LOG2E, scale_b=dk ** -0.5)
        gla = functools.partial(_gla, wgk_pad=wgk_pad, bgk=row(b_gk[l]), gout=row(g_gla_out[l]))

        def tail(x, oa, ob, p, n, mlp_weights):
            x = _outproj(oa, ob, x, w_out_b, tm=_tile(n, 512))
            if mlp_weights[0].dtype == bf16:
                x = _ffn(x, row(g_mlp[l]), *mlp_weights, tm=_tile(n, 512), tf=1024)
            else:
                x, *mlp_weights = _ffn(x, row(g_mlp[l]), *mlp_weights, tm=n, tf=512)
            x = _ple(x, p.reshape(n, -1), w_ple_b, row(g_ple[l]), row(g_ple_gate[l]), w_gate_b,
                     row(g_final), tm=_tile(n, 512), final=final)
            return x, mlp_weights

        qa, kab, vab, qb, kb, vb, rb, ka, va, fg = inproj(xs, tm=_tile(n_s, 256))
        logf, _, _ = _fox_prep(fg, bf_pad, batch=1, nh=nh)
        key_rows = lambda a: a.reshape(bs, ts * nh, hd)
        lfnew = jnp.pad(logf.reshape(bs, 1, ts * nh), ((0, 0), (0, 0), (0, LANE - ts * nh)))
        oa = _fox_decode(page_table, key_rows(qa), key_rows(kab), key_rows(vab), lfnew,
                         cache_k[l].reshape(n_pool, page * nh, hd), cache_v[l].reshape(n_pool, page * nh, hd),
                         cache_logf[l].reshape(n_pool, 1, page * nh), nh=nh, ppc=4)
        oa = oa.reshape(n_s, w_a).astype(bf16)
        tpad = 16
        flat = lambda a: jnp.pad(a.reshape(bs, ts, -1), ((0, 0), (0, tpad - ts), (0, 0))).reshape(bs * tpad, -1)
        ob, s_s = gla(flat(qb), flat(kb), flat(vb), flat(rb), flat(fg), s0=state_gla[l], batch=bs,
                      tc=tpad, n_seq=_tile(bs, 8), chunk=tpad, t_valid=ts)
        ob = ob.reshape(bs, tpad, wv_b)[:, :ts].reshape(n_s, wv_b)
        xs, mlp_weights = tail(xs, oa, ob, p_sample[l], n_s, (w_up[l], w_down[l]))
        outs["ks"].append(ka.reshape(bs, ts, nh, hd))
        outs["vs"].append(va.reshape(bs, ts, nh, hd))
        outs["fs"].append(logf.reshape(bs, ts, nh))
        outs["ss"].append(s_s)

        qa, kab, vab, qb, kb, vb, rb, ka, va, fg = inproj(xp, tm=_tile(n_p, 256))
        logf, c, ct = _fox_prep(fg, bf_pad, batch=batch, nh=nh)
        oa = _fox_attn(qa, kab, vab, c, ct, batch=batch, nh=nh, hd=hd, tq=_tile(seq, 256))
        ob, s_p = gla(qb, kb, vb, rb, fg, s0=jnp.zeros((batch, nh_b, dk, dv), f32), batch=batch,
                      tc=_tile(seq, 1024), n_seq=1, chunk=GLA_CHUNK, t_valid=None)
        xp, _ = tail(xp, oa, ob, p_prompt[l], n_p, mlp_weights)
        outs["kp"].append(ka.reshape(batch, seq, nh, hd))
        outs["vp"].append(va.reshape(batch, seq, nh, hd))
        outs["fp"].append(logf.reshape(batch, seq, nh))
        outs["sp"].append(s_p)

    st = lambda key: jnp.stack(outs[key])
    return (xp.reshape(batch, seq, d), xs.reshape(bs, ts, d), st("kp"), st("vp"), st("fp"), st("sp"),
            st("ks"), st("vs"), st("fs"), st("ss"))
```

```python
import functools

import jax
import jax.numpy as jnp
from jax import lax
from jax.experimental import pallas as pl
from jax.experimental.pallas import tpu as pltpu

f32 = jnp.float32
bf16 = jnp.bfloat16

EPS = 1e-6
NEG = -1e30
GK_NORM = 16.0
GLA_CHUNK = 64
LOG2E = 1.4426950408889634

LANE = 128
SUBLANE = 8
MIB = 2**20

_NT = (((1,), (1,)), ((), ()))
_TN = (((0,), (0,)), ((), ()))


def _params(semantics, vmem_mib):
    return pltpu.CompilerParams(dimension_semantics=semantics, vmem_limit_bytes=vmem_mib * MIB)


def _resident(shape):
    return pl.BlockSpec(shape, lambda *_: (0,) * len(shape), pipeline_mode=pl.Buffered(1))


def _rms(x, g):
    return x * lax.rsqrt(jnp.mean(x * x, axis=-1, keepdims=True) + EPS) * g


def _log_sigmoid(x):
    return jnp.minimum(x, 0.0) - jnp.log1p(jnp.exp(-jnp.abs(x)))


def _split3(x):
    hi = x.astype(bf16)
    r = x - hi.astype(f32)
    mid = r.astype(bf16)
    lo = (r - mid.astype(f32)).astype(bf16)
    return hi, mid, lo


def _dot(a, b):
    return jnp.dot(a, b, preferred_element_type=f32)


def _dot_sel_left(sel, x):
    return sum(_dot(sel, t) for t in _split3(x))


def _tri(n):
    r = lax.broadcasted_iota(jnp.int32, (n, n), 0)
    c = lax.broadcasted_iota(jnp.int32, (n, n), 1)
    return (r >= c).astype(bf16)


def _repack_kernel(wt_hbm, main_ref, fg_ref, buf, small, sem, small_sem, *, o_f, o_b, o_g, tn):
    j = pl.program_id(0)
    n_first = o_f // tn

    def rows_copy(jj, slot):
        start = jnp.where(jj < n_first, jj * tn, o_b + (jj - n_first) * tn)
        return pltpu.make_async_copy(wt_hbm.at[pl.ds(pl.multiple_of(start, SUBLANE), tn)], buf.at[slot], sem.at[slot])

    @pl.when(j == 0)
    def _():
        rows_copy(0, 0).start()

    @pl.when(j + 1 < pl.num_programs(0))
    def _():
        rows_copy(j + 1, (j + 1) % 2).start()

    @pl.when(j == 0)
    def _():
        small[...] = jnp.zeros(small.shape, f32)
        n_f, n_g = o_b - o_f, wt_hbm.shape[0] - o_g
        copies = [pltpu.make_async_copy(wt_hbm.at[pl.ds(o_f, n_f)], small.at[pl.ds(0, n_f)], small_sem.at[0]),
                  pltpu.make_async_copy(wt_hbm.at[pl.ds(o_g, n_g)], small.at[pl.ds(n_f, n_g)], small_sem.at[1])]
        for cp in copies:
            cp.start()
        for cp in copies:
            cp.wait()
        fg_ref[...] = small[...].T.astype(bf16)

    rows_copy(j, j % 2).wait()
    main_ref[...] = buf[j % 2].T.astype(bf16)


def _repack_w_in(w, *, o_f, o_b, o_g, tn):
    d, cols = w.shape
    n_main = o_f + o_g - o_b
    assert o_f % tn == 0 and (o_g - o_b) % tn == 0 and o_b % SUBLANE == 0 and (o_b - o_f) + (cols - o_g) <= LANE
    sds = jax.ShapeDtypeStruct
    return pl.pallas_call(
        functools.partial(_repack_kernel, o_f=o_f, o_b=o_b, o_g=o_g, tn=tn),
        grid=(n_main // tn,),
        in_specs=[pl.BlockSpec(memory_space=pl.ANY)],
        out_specs=[pl.BlockSpec((d, tn), lambda j: (0, j)), pl.BlockSpec((d, LANE), lambda j: (0, 0))],
        out_shape=[sds((d, n_main), bf16), sds((d, LANE), bf16)],
        scratch_shapes=[pltpu.VMEM((2, tn, d), f32), pltpu.VMEM((LANE, d), f32),
                        pltpu.SemaphoreType.DMA((2,)), pltpu.SemaphoreType.DMA((2,))],
        compiler_params=_params(("arbitrary",), 32),
        name="repack_w_in",
    )(jnp.swapaxes(w, 0, 1))


def _inproj_kernel(x_ref, g_ref, w_ref, wfg_ref,
                   qa_ref, kab_ref, vab_ref, qb_ref, kb_ref, vb_ref, rb_ref, ka_ref, va_ref, fg_ref,
                   *, w_a, wk_b, wv_b, scale_a, scale_b, chunk):
    h = _rms(x_ref[...], g_ref[...]).astype(bf16)
    col = 0

    def section(width, store):
        nonlocal col
        for j in range(0, width, chunk):
            store(j, _dot(h, w_ref[:, col + j:col + j + chunk]))
        col += width

    def st_qa(j, z):
        qa_ref[:, j:j + chunk] = (z * scale_a).astype(bf16)

    def st_ka(j, z):
        ka_ref[:, j:j + chunk] = z
        kab_ref[:, j:j + chunk] = z.astype(bf16)

    def st_va(j, z):
        va_ref[:, j:j + chunk] = z
        vab_ref[:, j:j + chunk] = z.astype(bf16)

    def st_qb(j, z):
        qb_ref[:, j:j + chunk] = (z * scale_b).astype(bf16)

    def st_kb(j, z):
        kb_ref[:, j:j + chunk] = z.astype(bf16)

    def st_vb(j, z):
        vb_ref[:, j:j + chunk] = z.astype(bf16)

    def st_rb(j, z):
        rb_ref[:, j:j + chunk] = z.astype(bf16)

    section(w_a, st_qa)
    section(w_a, st_ka)
    section(w_a, st_va)
    section(wk_b, st_qb)
    section(wk_b, st_kb)
    section(wv_b, st_vb)
    section(wv_b, st_rb)
    fg_ref[...] = _dot(h, wfg_ref[...])


def _inproj(x, g, w_main, w_fg, *, nh, hd, wk_b, wv_b, scale_a, scale_b, tm):
    n, d = x.shape
    w_a = nh * hd
    row = lambda width: pl.BlockSpec((tm, width), lambda i: (i, 0))
    sds = jax.ShapeDtypeStruct
    kern = functools.partial(_inproj_kernel, w_a=w_a, wk_b=wk_b, wv_b=wv_b,
                             scale_a=scale_a, scale_b=scale_b, chunk=512)
    return pl.pallas_call(
        kern,
        grid=(n // tm,),
        in_specs=[row(d), _resident((1, d)), _resident(w_main.shape), _resident((d, LANE))],
        out_specs=[row(w_a), row(w_a), row(w_a), row(wk_b), row(wk_b), row(wv_b), row(wv_b),
                   row(w_a), row(w_a), row(LANE)],
        out_shape=[sds((n, w_a), bf16), sds((n, w_a), bf16), sds((n, w_a), bf16),
                   sds((n, wk_b), bf16), sds((n, wk_b), bf16), sds((n, wv_b), bf16), sds((n, wv_b), bf16),
                   sds((n, w_a), f32), sds((n, w_a), f32), sds((n, LANE), f32)],
        compiler_params=_params(("parallel",), 52),
        name="inproj",
    )(x, g, w_main, w_fg)


def _fox_prep_kernel(fg_ref, bf_ref, logf_ref, logft_ref, c_ref, ct_ref, *, nh, blk):
    t = fg_ref.shape[0]
    tri = _tri(blk)
    carry = jnp.zeros((1, LANE), f32)
    for b in range(t // blk):
        rows = slice(b * blk, (b + 1) * blk)
        logf = _log_sigmoid(fg_ref[rows, :] + bf_ref[...])
        logf_ref[rows, :] = logf[:, :nh]
        logft_ref[0, :, rows] = logf.T[:nh, :]
        c = _dot_sel_left(tri, logf) + carry
        carry = c[blk - 1:blk, :]
        c2 = c * LOG2E
        c_ref[rows, :] = c2
        ct_ref[0, :, rows] = c2.T[:nh, :]


def _fox_prep(fg, bf_pad, *, batch, nh):
    n = fg.shape[0]
    t = n // batch
    blk = min(t, 256)
    sds = jax.ShapeDtypeStruct
    return pl.pallas_call(
        functools.partial(_fox_prep_kernel, nh=nh, blk=blk),
        grid=(batch,),
        in_specs=[pl.BlockSpec((t, LANE), lambda b: (b, 0)), _resident((1, LANE))],
        out_specs=[pl.BlockSpec((t, nh), lambda b: (b, 0)),
                   pl.BlockSpec((1, nh, t), lambda b: (b, 0, 0)),
                   pl.BlockSpec((t, LANE), lambda b: (b, 0)),
                   pl.BlockSpec((1, nh, t), lambda b: (b, 0, 0))],
        out_shape=[sds((n, nh), f32), sds((batch, nh, t), f32), sds((n, LANE), f32), sds((batch, nh, t), f32)],
        compiler_params=_params(("parallel",), 32),
        name="fox_prep",
    )(fg, bf_pad)


def _fox_attn_kernel(q_ref, k_ref, v_ref, c_ref, ct_ref, o_ref, m_scr, l_scr, acc_scr, cq_scr, *, nh, hd, tq):
    i = pl.program_id(1)
    row = lax.broadcasted_iota(jnp.int32, (tq, tq), 0)
    col = lax.broadcasted_iota(jnp.int32, (tq, tq), 1)
    causal = col <= row
    m_scr[...] = jnp.full(m_scr.shape, -jnp.inf, f32)
    l_scr[...] = jnp.zeros(l_scr.shape, f32)
    acc_scr[...] = jnp.zeros(acc_scr.shape, f32)
    for h in range(nh):
        cq_scr[h] = jnp.broadcast_to(c_ref[:, h:h + 1], (tq, LANE))

    def weights(h, t):
        m, cq = m_scr[h], cq_scr[h]
        m_new = jnp.maximum(m, jnp.max(t, axis=-1, keepdims=True) + cq)
        alpha = jnp.exp2(m - m_new)
        p = jnp.exp2(t + jnp.concatenate([cq - m_new] * (t.shape[1] // LANE), axis=1))
        l_scr[h] = alpha * l_scr[h] + jnp.sum(p, axis=-1, keepdims=True)
        m_scr[h] = m_new
        return p.astype(bf16), alpha

    def block(start, width, diag):
        keys = pl.ds(pl.multiple_of(start, width), width)
        for h0 in range(0, nh, 2):
            ps, alphas = [], []
            for h in (h0, h0 + 1):
                lanes = slice(h * hd, (h + 1) * hd)
                t = lax.dot_general(q_ref[:, lanes], k_ref[keys, lanes], _NT, preferred_element_type=f32)
                t = t - ct_ref[0, h:h + 1, keys]
                if diag:
                    t = jnp.where(causal, t, NEG)
                p, alpha = weights(h, t)
                ps.append(p)
                alphas.append(alpha)
            pv = _dot(jnp.concatenate(ps, axis=0), v_ref[keys, h0 * hd:(h0 + 2) * hd])
            acc_scr[h0] = alphas[0] * acc_scr[h0] + pv[:tq, :hd]
            acc_scr[h0 + 1] = alphas[1] * acc_scr[h0 + 1] + pv[tq:, hd:]

    def off_diagonal_pair(j, carry):
        block(j * 2 * tq, 2 * tq, False)
        return carry

    lax.fori_loop(0, i // 2, off_diagonal_pair, 0)

    @pl.when(i % 2 == 1)
    def _():
        block((i - 1) * tq, tq, False)

    block(i * tq, tq, True)
    for h in range(nh):
        o_ref[:, h * hd:(h + 1) * hd] = (acc_scr[h] / l_scr[h]).astype(bf16)


def _fox_attn(qa, kab, vab, c, ct, *, batch, nh, hd, tq):
    n, w_a = qa.shape
    t = n // batch
    nq = t // tq
    assert hd == LANE and nh % 2 == 0 and tq % LANE == 0
    return pl.pallas_call(
        functools.partial(_fox_attn_kernel, nh=nh, hd=hd, tq=tq),
        grid=(batch, nq),
        in_specs=[pl.BlockSpec((tq, w_a), lambda b, i: (b * nq + i, 0)),
                  pl.BlockSpec((t, w_a), lambda b, i: (b, 0)),
                  pl.BlockSpec((t, w_a), lambda b, i: (b, 0)),
                  pl.BlockSpec((tq, LANE), lambda b, i: (b * nq + i, 0)),
                  pl.BlockSpec((1, nh, t), lambda b, i: (b, 0, 0))],
        out_specs=pl.BlockSpec((tq, w_a), lambda b, i: (b * nq + i, 0)),
        out_shape=jax.ShapeDtypeStruct((n, w_a), bf16),
        scratch_shapes=[pltpu.VMEM((nh, tq, LANE), f32)] * 4,
        compiler_params=_params(("parallel", "arbitrary"), 40),
        name="fox_attn",
    )(qa, kab, vab, c, ct)


def _gla_kernel(q_ref, k_ref, v_ref, r_ref, fg_ref, wgk_ref, bgk_ref, gout_ref, s0_ref,
                o_ref, st_ref, s_scr, *, nh, dk, dv, chunk, n_chunks, n_seq, t_valid):
    t = pl.program_id(1)

    @pl.when(t == 0)
    def _():
        s_scr[...] = s0_ref[...]

    tri = _tri(chunk)
    row = lax.broadcasted_iota(jnp.int32, (chunk, chunk), 0)
    col = lax.broadcasted_iota(jnp.int32, (chunk, chunk), 1)
    causal = col <= row
    items = [(e, ci) for e in range(n_seq) for ci in range(n_chunks)]
    rows_of = lambda it: slice((it[0] * n_chunks + it[1]) * chunk, (it[0] * n_chunks + it[1] + 1) * chunk)
    heads = [(slice(h * dk, (h + 1) * dk), slice(h * dv, (h + 1) * dv)) for h in range(nh)]

    def stage1(it):
        la = _log_sigmoid(_dot(fg_ref[rows_of(it), :].astype(bf16), wgk_ref[...]) + bgk_ref[...]) / GK_NORM
        if t_valid is not None:
            la = jnp.where(lax.broadcasted_iota(jnp.int32, la.shape, 0) < t_valid, la, 0.0)
        return la

    def stage2(la):
        b = _dot_sel_left(tri, la)
        b_last = b[chunk - 1:chunk, :]
        return jnp.exp(b), jnp.exp(-b), jnp.exp(b_last - b), jnp.exp(b_last)

    def stage3(it, exps):
        e_pos, e_neg, e_rem, e_last = exps
        rows = rows_of(it)
        out = []
        for kl, _ in heads:
            q = q_ref[rows, kl].astype(f32)
            k = k_ref[rows, kl].astype(f32)
            qe = (q * e_pos[:, kl]).astype(bf16)
            ke = (k * e_neg[:, kl]).astype(bf16)
            kd = (k * e_rem[:, kl]).astype(bf16)
            a = lax.dot_general(qe, ke, _NT, preferred_element_type=f32)
            a = jnp.where(causal, a, 0.0).astype(bf16)
            decay = jnp.broadcast_to(e_last[:, kl], (SUBLANE, dk)).T[:, :1]
            out.append((qe, kd, a, decay))
        return out

    def stage4(it, per_head):
        rows = rows_of(it)
        for h, ((_, vl), (qe, kd, a, decay)) in enumerate(zip(heads, per_head)):
            v = v_ref[rows, vl]
            s = s_scr[it[0], h]
            o = _dot(qe, s.astype(bf16)) + _dot(a, v)
            s_scr[it[0], h] = s * decay + lax.dot_general(kd, v, _TN, preferred_element_type=f32)
            gated = _rms(o, gout_ref[...]) * jax.nn.silu(r_ref[rows, vl].astype(f32))
            o_ref[rows, vl] = gated.astype(bf16)

    r1, r2, r3 = {}, {}, {}
    for tick in range(len(items) + 3):
        if 0 <= tick - 3 < len(items):
            stage4(items[tick - 3], r3.pop(tick - 3))
        if 0 <= tick - 2 < len(items):
            r3[tick - 2] = stage3(items[tick - 2], r2.pop(tick - 2))
        if 0 <= tick - 1 < len(items):
            r2[tick - 1] = stage2(r1.pop(tick - 1))
        if tick < len(items):
            r1[tick] = stage1(items[tick])

    @pl.when(t == pl.num_programs(1) - 1)
    def _():
        st_ref[...] = s_scr[...]


def _gla(qb, kb, vb, rb, fg, wgk_pad, bgk, gout, s0, *, batch, tc, n_seq, chunk, t_valid):
    n, wk_b = qb.shape
    wv_b = vb.shape[1]
    _, nh, dk, dv = s0.shape
    nt = n // batch // tc
    assert n_seq == 1 or nt == 1
    rowspec = lambda width: pl.BlockSpec((n_seq * tc, width), lambda b, t: (b * nt + t, 0))
    state = pl.BlockSpec((n_seq, nh, dk, dv), lambda b, t: (b, 0, 0, 0))
    kern = functools.partial(_gla_kernel, nh=nh, dk=dk, dv=dv, chunk=chunk, n_chunks=tc // chunk, n_seq=n_seq,
                             t_valid=t_valid)
    return pl.pallas_call(
        kern,
        grid=(batch // n_seq, nt),
        in_specs=[rowspec(wk_b), rowspec(wk_b), rowspec(wv_b), rowspec(wv_b), rowspec(LANE),
                  _resident((LANE, wk_b)), _resident((1, wk_b)), _resident((1, dv)), state],
        out_specs=[rowspec(wv_b), state],
        out_shape=[jax.ShapeDtypeStruct((n, wv_b), bf16), jax.ShapeDtypeStruct(s0.shape, f32)],
        scratch_shapes=[pltpu.VMEM((n_seq, nh, dk, dv), f32)],
        compiler_params=_params(("parallel", "arbitrary"), 32),
        name="gla",
    )(qb, kb, vb, rb, fg, wgk_pad, bgk, gout, s0)


def _outproj_kernel(oa_ref, ob_ref, x_ref, w_ref, x1_ref, *, w_a):
    mixed = _dot(oa_ref[...], w_ref[:w_a, :]) + _dot(ob_ref[...], w_ref[w_a:, :])
    x1_ref[...] = x_ref[...] + mixed


def _outproj(oa, ob, x, w_out, *, tm):
    n, d = x.shape
    w_a, wv_b = oa.shape[1], ob.shape[1]
    row = lambda width: pl.BlockSpec((tm, width), lambda i: (i, 0))
    return pl.pallas_call(
        functools.partial(_outproj_kernel, w_a=w_a),
        grid=(n // tm,),
        in_specs=[row(w_a), row(wv_b), row(d), _resident((w_a + wv_b, d))],
        out_specs=row(d),
        out_shape=jax.ShapeDtypeStruct((n, d), f32),
        compiler_params=_params(("parallel",), 40),
        name="outproj",
    )(oa, ob, x, w_out)


def _ffn_kernel(x_ref, g_ref, wup_ref, wdn_ref, o_ref, *rest):
    h_scr = rest[-1]

    @pl.when(pl.program_id(1) == 0)
    def _():
        x = x_ref[...]
        h_scr[...] = _rms(x, g_ref[...]).astype(bf16)
        o_ref[...] = x

    w_up, w_down = wup_ref[...].astype(bf16), wdn_ref[...].astype(bf16)
    if len(rest) == 3:
        rest[0][...] = w_up
        rest[1][...] = w_down
    a = _dot(h_scr[...], w_up)
    a = jnp.square(jnp.maximum(a, 0.0)).astype(bf16)
    o_ref[...] += _dot(a, w_down)


def _ffn(x, g, w_up, w_down, *, tm, tf):
    n, d = x.shape
    dff = w_up.shape[1]
    emit_cast = w_up.dtype != bf16
    assert not emit_cast or n == tm
    x_spec = pl.BlockSpec((tm, d), lambda i, f: (i, 0))
    up_spec = pl.BlockSpec((d, tf), lambda i, f: (0, f))
    down_spec = pl.BlockSpec((tf, d), lambda i, f: (f, 0))
    sds = jax.ShapeDtypeStruct
    out = pl.pallas_call(
        _ffn_kernel,
        grid=(n // tm, dff // tf),
        in_specs=[x_spec, _resident((1, d)), up_spec, down_spec],
        out_specs=[x_spec] + ([up_spec, down_spec] if emit_cast else []),
        out_shape=[sds((n, d), f32)] + ([sds(w_up.shape, bf16), sds(w_down.shape, bf16)] if emit_cast else []),
        scratch_shapes=[pltpu.VMEM((tm, d), bf16)],
        compiler_params=_params(("parallel", "arbitrary"), 52),
        name="ffn",
    )(x, g, w_up, w_down)
    return out if emit_cast else out[0]


def _ple_kernel(x_ref, p_ref, wple_ref, gple_ref, ggate_ref, wgate_ref, gfin_ref, y_ref, *, final):
    x = x_ref[...]
    e = _rms(_dot(p_ref[...].astype(bf16), wple_ref[...]), gple_ref[...])
    gate = jax.nn.sigmoid(_dot(_rms(x, ggate_ref[...]).astype(bf16), wgate_ref[...]))
    x = x + gate * e
    y_ref[...] = _rms(x, gfin_ref[...]) if final else x


def _ple(x, p, w_ple, g_ple, g_gate, w_gate, g_final, *, tm, final):
    n, d = x.shape
    pd = p.shape[1]
    row = lambda width: pl.BlockSpec((tm, width), lambda i: (i, 0))
    return pl.pallas_call(
        functools.partial(_ple_kernel, final=final),
        grid=(n // tm,),
        in_specs=[row(d), row(pd), _resident((pd, d)), _resident((1, d)), _resident((1, d)),
                  _resident((d, d)), _resident((1, d))],
        out_specs=row(d),
        out_shape=jax.ShapeDtypeStruct((n, d), f32),
        compiler_params=_params(("parallel",), 40),
        name="ple",
    )(x, p, w_ple, g_ple, g_gate, w_gate, g_final)


def _scan_tokens(x, nh, n_tokens):
    lane = lax.broadcasted_iota(jnp.int32, x.shape, 1)
    shift = nh
    while shift < n_tokens * nh:
        x = x + jnp.where(lane >= shift, pltpu.roll(x, shift, axis=1), 0.0)
        shift *= 2
    return x


def _spread_last_group(x, nh):
    shift = nh
    while shift < LANE:
        x = x + pltpu.roll(x, shift, axis=1)
        shift *= 2
    return x


def _fox_decode_kernel(pt_ref, q_ref, knew_ref, vnew_ref, lfnew_ref, ck_hbm, cv_hbm, clf_hbm, o_ref,
                       kbuf, vbuf, lbuf, ksem, vsem, lsem, k2, v2, knew_scr, vnew_scr,
                       *, nh, hd, ts, n_pages, ppc):
    b = pl.program_id(0)
    prow = lbuf.shape[2]
    page = prow // nh
    hp = ppc // 2
    half = hp * prow
    rows_q = ts * nh

    def page_copies(hbm, buf, sem, rows, bb, slot):
        return [pltpu.make_async_copy(hbm.at[pt_ref[bb, p]], buf.at[slot, pl.ds(p * rows, rows)], sem.at[slot])
                for p in range(n_pages)]

    def all_copies(bb, slot):
        return (page_copies(ck_hbm, kbuf, ksem, prow, bb, slot) + page_copies(cv_hbm, vbuf, vsem, prow, bb, slot)
                + page_copies(clf_hbm, lbuf, lsem, 1, bb, slot))

    @pl.when(b == 0)
    def _():
        for cp in all_copies(0, 0):
            cp.start()
        knew_scr[...] = jnp.zeros(knew_scr.shape, bf16)
        vnew_scr[...] = jnp.zeros(vnew_scr.shape, bf16)

    @pl.when(b + 1 < pl.num_programs(0))
    def _():
        for cp in all_copies(b + 1, (b + 1) % 2):
            cp.start()

    slot = b % 2
    for cp in all_copies(b, slot):
        cp.wait()

    within = _scan_tokens(lbuf[slot], nh, page)
    lane = lax.broadcasted_iota(jnp.int32, (n_pages, LANE), 1)
    totals = jnp.where(lane >= LANE - nh, within[:, prow - LANE:], 0.0)
    pr = lax.broadcasted_iota(jnp.int32, (n_pages, n_pages), 0)
    pc = lax.broadcasted_iota(jnp.int32, (n_pages, n_pages), 1)
    before = _dot_sel_left((pr > pc).astype(bf16), totals)
    cflat = (within + jnp.concatenate([_spread_last_group(before, nh)] * (prow // LANE), axis=1)) * LOG2E
    whole = _spread_last_group(before[n_pages - 1:, :] + totals[n_pages - 1:, :], nh)
    c_new = (_scan_tokens(lfnew_ref[0], nh, LANE // nh) + whole) * LOG2E
    rr = lax.broadcasted_iota(jnp.int32, (rows_q, LANE), 0)
    ll = lax.broadcasted_iota(jnp.int32, (rows_q, LANE), 1)
    cq = jnp.sum(jnp.where(ll == rr, jnp.broadcast_to(c_new, (rows_q, LANE)), 0.0),
                 axis=-1, keepdims=True)
    q = q_ref[0]
    zero = jnp.zeros(q.shape, bf16)
    q2 = jnp.concatenate([jnp.concatenate([q, zero], axis=1), jnp.concatenate([zero, q], axis=1)], axis=0)

    def both(x):
        return jnp.concatenate([x, x], axis=0)

    m = jnp.full((rows_q, 1), -jnp.inf, f32)
    l = jnp.zeros((rows_q, 1), f32)
    acc = jnp.zeros((2 * rows_q, 2 * hd), f32)
    for blk in range(n_pages // ppc):
        row_a = blk * ppc * prow
        k2[blk, :, :hd] = kbuf[slot, row_a:row_a + half].astype(bf16)
        k2[blk, :, hd:] = kbuf[slot, row_a + half:row_a + 2 * half].astype(bf16)
        s = lax.dot_general(q2, k2[blk], _NT, preferred_element_type=f32)
        pg = blk * ppc
        bias = jnp.concatenate(
            [jnp.concatenate([jnp.broadcast_to(cflat[pg + p:pg + p + 1, :], (rows_q, prow)),
                              jnp.broadcast_to(cflat[pg + hp + p:pg + hp + p + 1, :], (rows_q, prow))], axis=0)
             for p in range(hp)], axis=1)
        r = lax.broadcasted_iota(jnp.int32, s.shape, 0)
        n = lax.broadcasted_iota(jnp.int32, s.shape, 1)
        t = jnp.where(n % nh == r % nh, s - bias, NEG)
        rowmax = jnp.max(t, axis=-1, keepdims=True)
        m_new = jnp.maximum(m, jnp.maximum(rowmax[:rows_q], rowmax[rows_q:]) + cq)
        alpha = jnp.exp2(m - m_new)
        p = jnp.exp2(t + both(cq - m_new))
        rowsum = jnp.sum(p, axis=-1, keepdims=True)
        l = alpha * l + rowsum[:rows_q] + rowsum[rows_q:]
        m = m_new
        v2[blk, :, :hd] = vbuf[slot, row_a:row_a + half].astype(bf16)
        v2[blk, :, hd:] = vbuf[slot, row_a + half:row_a + 2 * half].astype(bf16)
        acc = both(alpha) * acc + _dot(p.astype(bf16), v2[blk])

    knew_scr[:rows_q, :] = knew_ref[0]
    vnew_scr[:rows_q, :] = vnew_ref[0]
    s = lax.dot_general(q, knew_scr[...], _NT, preferred_element_type=f32)
    r = lax.broadcasted_iota(jnp.int32, s.shape, 0)
    n = lax.broadcasted_iota(jnp.int32, s.shape, 1)
    valid = jnp.logical_and(n % nh == r % nh, n // nh <= r // nh)
    t = jnp.where(valid, s - c_new, NEG)
    m_new = jnp.maximum(m, jnp.max(t, axis=-1, keepdims=True) + cq)
    alpha = jnp.exp2(m - m_new)
    p = jnp.exp2(t + (cq - m_new))
    l = alpha * l + jnp.sum(p, axis=-1, keepdims=True)
    acc = both(alpha) * acc
    o = acc[:rows_q, :hd] + acc[rows_q:, hd:] + _dot(p.astype(bf16), vnew_scr[...])
    o_ref[0] = o / l


def _fox_decode(page_table, q, knew, vnew, lfnew, cache_k, cache_v, cache_lf, *, nh, ppc):
    bs, rows_q, hd = q.shape
    ts = rows_q // nh
    n_pages = page_table.shape[1]
    prow = cache_k.shape[1]
    assert rows_q <= LANE and hd == LANE and prow % LANE == 0 and LANE % nh == 0
    assert ppc % 2 == 0 and n_pages % ppc == 0
    any_spec = pl.BlockSpec(memory_space=pl.ANY)
    per_b = lambda shape: pl.BlockSpec((1,) + shape, lambda s, pt: (s, 0, 0))
    half = ppc // 2 * prow
    grid_spec = pltpu.PrefetchScalarGridSpec(
        num_scalar_prefetch=1,
        grid=(bs,),
        in_specs=[per_b((rows_q, hd)), per_b((rows_q, hd)), per_b((rows_q, hd)), per_b((1, LANE)),
                  any_spec, any_spec, any_spec],
        out_specs=per_b((rows_q, hd)),
        scratch_shapes=[
            pltpu.VMEM((2, n_pages * prow, hd), f32), pltpu.VMEM((2, n_pages * prow, hd), f32),
            pltpu.VMEM((2, n_pages, prow), f32),
            pltpu.SemaphoreType.DMA((2,)), pltpu.SemaphoreType.DMA((2,)), pltpu.SemaphoreType.DMA((2,)),
            pltpu.VMEM((n_pages // ppc, half, 2 * hd), bf16), pltpu.VMEM((n_pages // ppc, half, 2 * hd), bf16),
            pltpu.VMEM((LANE, hd), bf16), pltpu.VMEM((LANE, hd), bf16),
        ],
    )
    kern = functools.partial(_fox_decode_kernel, nh=nh, hd=hd, ts=ts, n_pages=n_pages, ppc=ppc)
    return pl.pallas_call(
        kern,
        grid_spec=grid_spec,
        out_shape=jax.ShapeDtypeStruct((bs, rows_q, hd), f32),
        compiler_params=_params(("arbitrary",), 56),
        name="fox_decode",
    )(page_table, q, knew, vnew, lfnew, cache_k, cache_v, cache_lf)


def _tile(n, pref):
    while n % pref:
        pref //= 2
    return pref


def kernel(x_prompt, x_sample, cache_k, cache_v, cache_logf, state_gla, page_table, p_prompt, p_sample,
           g_mix, w_in, b_f, w_gk2, b_gk, g_gla_out, w_out, g_mlp, w_up, w_down, w_ple, g_ple, g_ple_gate,
           w_ple_gate, g_final):
    batch, seq, d = x_prompt.shape
    bs, ts, _ = x_sample.shape
    depth = g_mix.shape[0]
    _, n_pool, page, nh, hd = cache_k.shape
    _, _, nh_b, dk, dv = state_gla.shape
    w_a, wk_b, wv_b = nh * hd, nh_b * dk, nh_b * dv
    rank = w_gk2.shape[1]
    n_p, n_s = batch * seq, bs * ts
    assert ts <= SUBLANE and ts % GLA_CHUNK != 0 and seq % GLA_CHUNK == 0
    row = lambda a: a.reshape(1, -1)

    xp = x_prompt.reshape(n_p, d)
    xs = x_sample.reshape(n_s, d)
    outs = {k: [] for k in ("kp", "vp", "fp", "sp", "ks", "vs", "fs", "ss")}
    for l in range(depth):
        final = l == depth - 1
        o_f = 3 * w_a
        o_b = o_f + nh
        o_g = o_b + 2 * wk_b + 2 * wv_b
        w_main, w_fg = _repack_w_in(w_in[l], o_f=o_f, o_b=o_b, o_g=o_g, tn=256)
        bf_pad = jnp.zeros((1, LANE), f32).at[0, :nh].set(b_f[l])
        wgk_pad = jnp.zeros((LANE, wk_b), f32).at[nh:nh + rank].set(w_gk2[l]).astype(bf16)
        w_out_b, w_ple_b, w_gate_b = w_out[l].astype(bf16), w_ple[l].astype(bf16), w_ple_gate[l].astype(bf16)
        inproj = functools.partial(_inproj, g=row(g_mix[l]), w_main=w_main, w_fg=w_fg, nh=nh, hd=hd,
                                   wk_b=wk_b, wv_b=wv_b, scale_a=hd ** -0.5 * LOG2E, scale_b=dk ** -0.5)
        gla = functools.partial(_gla, wgk_pad=wgk_pad, bgk=row(b_gk[l]), gout=row(g_gla_out[l]))

        def tail(x, oa, ob, p, n, mlp_weights):
            x = _outproj(oa, ob, x, w_out_b, tm=_tile(n, 512))
            if mlp_weights[0].dtype == bf16:
                x = _ffn(x, row(g_mlp[l]), *mlp_weights, tm=_tile(n, 512), tf=1024)
            else:
                x, *mlp_weights = _ffn(x, row(g_mlp[l]), *mlp_weights, tm=n, tf=512)
            x = _ple(x, p.reshape(n, -1), w_ple_b, row(g_ple[l]), row(g_ple_gate[l]), w_gate_b,
                     row(g_final), tm=_tile(n, 512), final=final)
            return x, mlp_weights

        qa, kab, vab, qb, kb, vb, rb, ka, va, fg = inproj(xs, tm=_tile(n_s, 256))
        logf, _, _, _ = _fox_prep(fg, bf_pad, batch=1, nh=nh)
        key_rows = lambda a: a.reshape(bs, ts * nh, hd)
        lfnew = jnp.pad(logf.reshape(bs, 1, ts * nh), ((0, 0), (0, 0), (0, LANE - ts * nh)))
        oa = _fox_decode(page_table, key_rows(qa), key_rows(kab), key_rows(vab), lfnew,
                         cache_k[l].reshape(n_pool, page * nh, hd), cache_v[l].reshape(n_pool, page * nh, hd),
                         cache_logf[l].reshape(n_pool, 1, page * nh), nh=nh, ppc=4)
        oa = oa.reshape(n_s, w_a).astype(bf16)
        tpad = 16
        flat = lambda a: jnp.pad(a.reshape(bs, ts, -1), ((0, 0), (0, tpad - ts), (0, 0))).reshape(bs * tpad, -1)
        ob, s_s = gla(flat(qb), flat(kb), flat(vb), flat(rb), flat(fg), s0=state_gla[l], batch=bs,
                      tc=tpad, n_seq=_tile(bs, 8), chunk=tpad, t_valid=ts)
        ob = ob.reshape(bs, tpad, wv_b)[:, :ts].reshape(n_s, wv_b)
        xs, mlp_weights = tail(xs, oa, ob, p_sample[l], n_s, (w_up[l], w_down[l]))
        outs["ks"].append(ka.reshape(bs, ts, nh, hd))
        outs["vs"].append(va.reshape(bs, ts, nh, hd))
        outs["fs"].append(logf.reshape(bs, ts, nh))
        outs["ss"].append(s_s)

        qa, kab, vab, qb, kb, vb, rb, ka, va, fg = inproj(xp, tm=_tile(n_p, 256))
        _, logf_t, c, ct = _fox_prep(fg, bf_pad, batch=batch, nh=nh)
        oa = _fox_attn(qa, kab, vab, c, ct, batch=batch, nh=nh, hd=hd, tq=_tile(seq, 256))
        ob, s_p = gla(qb, kb, vb, rb, fg, s0=jnp.zeros((batch, nh_b, dk, dv), f32), batch=batch,
                      tc=_tile(seq, 1024), n_seq=1, chunk=GLA_CHUNK, t_valid=None)
        xp, _ = tail(xp, oa, ob, p_prompt[l], n_p, mlp_weights)
        outs["kp"].append(ka.reshape(batch, seq, nh, hd))
        outs["vp"].append(va.reshape(batch, seq, nh, hd))
        outs["fp"].append(jnp.swapaxes(logf_t, 1, 2))
        outs["sp"].append(s_p)

    st = lambda key: jnp.stack(outs[key])
    return (xp.reshape(batch, seq, d), xs.reshape(bs, ts, d), st("kp"), st("vp"), st("fp"), st("sp"),
            st("ks"), st("vs"), st("fs"), st("ss"))
```

```python
import functools

import jax
import jax.numpy as jnp
from jax import lax
from jax.experimental import pallas as pl
from jax.experimental.pallas import tpu as pltpu

f32 = jnp.float32
bf16 = jnp.bfloat16

EPS = 1e-6
NEG = -1e30
GK_NORM = 16.0
GLA_CHUNK = 64
LOG2E = 1.4426950408889634

LANE = 128
SUBLANE = 8
MIB = 2**20
V7X_VMEM_MIB = 64

_NT = (((1,), (1,)), ((), ()))
_TN = (((0,), (0,)), ((), ()))


def _params(semantics, vmem_mib):
    assert vmem_mib < V7X_VMEM_MIB
    return pltpu.CompilerParams(dimension_semantics=semantics, vmem_limit_bytes=vmem_mib * MIB)


def _resident(shape):
    return pl.BlockSpec(shape, lambda *_: (0,) * len(shape), pipeline_mode=pl.Buffered(1))


def _rms(x, g):
    return x * lax.rsqrt(jnp.mean(x * x, axis=-1, keepdims=True) + EPS) * g


def _log_sigmoid(x):
    return jnp.minimum(x, 0.0) - jnp.log1p(jnp.exp(-jnp.abs(x)))


def _split3(x):
    hi = x.astype(bf16)
    r = x - hi.astype(f32)
    mid = r.astype(bf16)
    lo = (r - mid.astype(f32)).astype(bf16)
    return hi, mid, lo


def _dot(a, b):
    return jnp.dot(a, b, preferred_element_type=f32)


def _dot_sel_left(sel, x):
    return sum(_dot(sel, t) for t in _split3(x))


def _tri(n):
    r = lax.broadcasted_iota(jnp.int32, (n, n), 0)
    c = lax.broadcasted_iota(jnp.int32, (n, n), 1)
    return (r >= c).astype(bf16)


def _repack_kernel(wt_hbm, main_ref, fg_ref, buf, small, sem, small_sem, *, o_f, o_b, o_g, tn):
    j = pl.program_id(0)
    n_first = o_f // tn

    def rows_copy(jj, slot):
        start = jnp.where(jj < n_first, jj * tn, o_b + (jj - n_first) * tn)
        return pltpu.make_async_copy(wt_hbm.at[pl.ds(pl.multiple_of(start, SUBLANE), tn)], buf.at[slot], sem.at[slot])

    @pl.when(j == 0)
    def _():
        rows_copy(0, 0).start()

    @pl.when(j + 1 < pl.num_programs(0))
    def _():
        rows_copy(j + 1, (j + 1) % 2).start()

    @pl.when(j == 0)
    def _():
        small[...] = jnp.zeros(small.shape, f32)
        n_f, n_g = o_b - o_f, wt_hbm.shape[0] - o_g
        copies = [pltpu.make_async_copy(wt_hbm.at[pl.ds(o_f, n_f)], small.at[pl.ds(0, n_f)], small_sem.at[0]),
                  pltpu.make_async_copy(wt_hbm.at[pl.ds(o_g, n_g)], small.at[pl.ds(n_f, n_g)], small_sem.at[1])]
        for cp in copies:
            cp.start()
        for cp in copies:
            cp.wait()
        fg_ref[...] = small[...].T.astype(bf16)

    rows_copy(j, j % 2).wait()
    main_ref[...] = buf[j % 2].T.astype(bf16)


def _repack_w_in(w, *, o_f, o_b, o_g, tn):
    d, cols = w.shape
    n_main = o_f + o_g - o_b
    assert o_f % tn == 0 and (o_g - o_b) % tn == 0 and o_b % SUBLANE == 0 and (o_b - o_f) + (cols - o_g) <= LANE
    sds = jax.ShapeDtypeStruct
    return pl.pallas_call(
        functools.partial(_repack_kernel, o_f=o_f, o_b=o_b, o_g=o_g, tn=tn),
        grid=(n_main // tn,),
        in_specs=[pl.BlockSpec(memory_space=pl.ANY)],
        out_specs=[pl.BlockSpec((d, tn), lambda j: (0, j)), pl.BlockSpec((d, LANE), lambda j: (0, 0))],
        out_shape=[sds((d, n_main), bf16), sds((d, LANE), bf16)],
        scratch_shapes=[pltpu.VMEM((2, tn, d), f32), pltpu.VMEM((LANE, d), f32),
                        pltpu.SemaphoreType.DMA((2,)), pltpu.SemaphoreType.DMA((2,))],
        compiler_params=_params(("arbitrary",), 32),
        name="repack_w_in",
    )(jnp.swapaxes(w, 0, 1))


def _inproj_kernel(x_ref, g_ref, w_ref, wfg_ref,
                   qa_ref, kab_ref, vab_ref, qb_ref, kb_ref, vb_ref, rb_ref, ka_ref, va_ref, fg_ref,
                   *, w_a, wk_b, wv_b, scale_a, scale_b, chunk):
    h = _rms(x_ref[...], g_ref[...]).astype(bf16)
    col = 0

    def section(width, store):
        nonlocal col
        for j in range(0, width, chunk):
            store(j, _dot(h, w_ref[:, col + j:col + j + chunk]))
        col += width

    def st_qa(j, z):
        qa_ref[:, j:j + chunk] = (z * scale_a).astype(bf16)

    def st_ka(j, z):
        ka_ref[:, j:j + chunk] = z
        kab_ref[:, j:j + chunk] = z.astype(bf16)

    def st_va(j, z):
        va_ref[:, j:j + chunk] = z
        vab_ref[:, j:j + chunk] = z.astype(bf16)

    def st_qb(j, z):
        qb_ref[:, j:j + chunk] = (z * scale_b).astype(bf16)

    def st_kb(j, z):
        kb_ref[:, j:j + chunk] = z.astype(bf16)

    def st_vb(j, z):
        vb_ref[:, j:j + chunk] = z.astype(bf16)

    def st_rb(j, z):
        rb_ref[:, j:j + chunk] = z.astype(bf16)

    section(w_a, st_qa)
    section(w_a, st_ka)
    section(w_a, st_va)
    section(wk_b, st_qb)
    section(wk_b, st_kb)
    section(wv_b, st_vb)
    section(wv_b, st_rb)
    fg_ref[...] = _dot(h, wfg_ref[...])


def _inproj(x, g, w_main, w_fg, *, nh, hd, wk_b, wv_b, scale_a, scale_b, tm):
    n, d = x.shape
    w_a = nh * hd
    row = lambda width: pl.BlockSpec((tm, width), lambda i: (i, 0))
    sds = jax.ShapeDtypeStruct
    kern = functools.partial(_inproj_kernel, w_a=w_a, wk_b=wk_b, wv_b=wv_b,
                             scale_a=scale_a, scale_b=scale_b, chunk=512)
    return pl.pallas_call(
        kern,
        grid=(n // tm,),
        in_specs=[row(d), _resident((1, d)), _resident(w_main.shape), _resident((d, LANE))],
        out_specs=[row(w_a), row(w_a), row(w_a), row(wk_b), row(wk_b), row(wv_b), row(wv_b),
                   row(w_a), row(w_a), row(LANE)],
        out_shape=[sds((n, w_a), bf16), sds((n, w_a), bf16), sds((n, w_a), bf16),
                   sds((n, wk_b), bf16), sds((n, wk_b), bf16), sds((n, wv_b), bf16), sds((n, wv_b), bf16),
                   sds((n, w_a), f32), sds((n, w_a), f32), sds((n, LANE), f32)],
        compiler_params=_params(("parallel",), 52),
        name="inproj",
    )(x, g, w_main, w_fg)


def _fox_prep_kernel(fg_ref, bf_ref, logf_ref, logft_ref, c_ref, ct_ref, *, nh, blk):
    t = fg_ref.shape[0]
    tri = _tri(blk)
    carry = jnp.zeros((1, LANE), f32)
    for b in range(t // blk):
        rows = slice(b * blk, (b + 1) * blk)
        logf = _log_sigmoid(fg_ref[rows, :] + bf_ref[...])
        logf_ref[rows, :] = logf[:, :nh]
        logft_ref[0, :, rows] = logf.T[:nh, :]
        c = _dot_sel_left(tri, logf) + carry
        carry = c[blk - 1:blk, :]
        c2 = c * LOG2E
        c_ref[rows, :] = c2
        ct_ref[0, :, rows] = c2.T[:nh, :]


def _fox_prep(fg, bf_pad, *, batch, nh):
    n = fg.shape[0]
    t = n // batch
    blk = min(t, 256)
    sds = jax.ShapeDtypeStruct
    return pl.pallas_call(
        functools.partial(_fox_prep_kernel, nh=nh, blk=blk),
        grid=(batch,),
        in_specs=[pl.BlockSpec((t, LANE), lambda b: (b, 0)), _resident((1, LANE))],
        out_specs=[pl.BlockSpec((t, nh), lambda b: (b, 0)),
                   pl.BlockSpec((1, nh, t), lambda b: (b, 0, 0)),
                   pl.BlockSpec((t, LANE), lambda b: (b, 0)),
                   pl.BlockSpec((1, nh, t), lambda b: (b, 0, 0))],
        out_shape=[sds((n, nh), f32), sds((batch, nh, t), f32), sds((n, LANE), f32), sds((batch, nh, t), f32)],
        compiler_params=_params(("parallel",), 32),
        name="fox_prep",
    )(fg, bf_pad)


def _fox_attn_kernel(q_ref, k_ref, v_ref, c_ref, ct_ref, o_ref, m_scr, l_scr, acc_scr, cq_scr, *, nh, hd, tq):
    i = pl.program_id(1)
    row = lax.broadcasted_iota(jnp.int32, (tq, tq), 0)
    col = lax.broadcasted_iota(jnp.int32, (tq, tq), 1)
    causal = col <= row
    m_scr[...] = jnp.full(m_scr.shape, -jnp.inf, f32)
    l_scr[...] = jnp.zeros(l_scr.shape, f32)
    acc_scr[...] = jnp.zeros(acc_scr.shape, f32)
    for h in range(nh):
        cq_scr[h] = jnp.broadcast_to(c_ref[:, h:h + 1], (tq, LANE))

    def weights(h, t):
        m, cq = m_scr[h], cq_scr[h]
        m_new = jnp.maximum(m, jnp.max(t, axis=-1, keepdims=True) + cq)
        alpha = jnp.exp2(m - m_new)
        p = jnp.exp2(t + jnp.concatenate([cq - m_new] * (t.shape[1] // LANE), axis=1))
        l_scr[h] = alpha * l_scr[h] + jnp.sum(p, axis=-1, keepdims=True)
        m_scr[h] = m_new
        return p.astype(bf16), alpha

    def block(start, width, diag):
        keys = pl.ds(pl.multiple_of(start, width), width)
        for h0 in range(0, nh, 2):
            ps, alphas = [], []
            for h in (h0, h0 + 1):
                lanes = slice(h * hd, (h + 1) * hd)
                t = lax.dot_general(q_ref[:, lanes], k_ref[keys, lanes], _NT, preferred_element_type=f32)
                t = t - ct_ref[0, h:h + 1, keys]
                if diag:
                    t = jnp.where(causal, t, NEG)
                p, alpha = weights(h, t)
                ps.append(p)
                alphas.append(alpha)
            pv = _dot(jnp.concatenate(ps, axis=0), v_ref[keys, h0 * hd:(h0 + 2) * hd])
            acc_scr[h0] = alphas[0] * acc_scr[h0] + pv[:tq, :hd]
            acc_scr[h0 + 1] = alphas[1] * acc_scr[h0 + 1] + pv[tq:, hd:]

    def off_diagonal_quad(j, carry):
        block(j * 4 * tq, 4 * tq, False)
        return carry

    lax.fori_loop(0, i // 4, off_diagonal_quad, 0)

    @pl.when(i % 4 >= 2)
    def _():
        block((i // 4) * 4 * tq, 2 * tq, False)

    @pl.when(i % 2 == 1)
    def _():
        block((i - 1) * tq, tq, False)

    block(i * tq, tq, True)
    for h in range(nh):
        o_ref[:, h * hd:(h + 1) * hd] = (acc_scr[h] / l_scr[h]).astype(bf16)


def _fox_attn(qa, kab, vab, c, ct, *, batch, nh, hd, tq):
    n, w_a = qa.shape
    t = n // batch
    nq = t // tq
    assert hd == LANE and nh % 2 == 0 and tq % LANE == 0
    return pl.pallas_call(
        functools.partial(_fox_attn_kernel, nh=nh, hd=hd, tq=tq),
        grid=(batch, nq),
        in_specs=[pl.BlockSpec((tq, w_a), lambda b, i: (b * nq + i, 0)),
                  pl.BlockSpec((t, w_a), lambda b, i: (b, 0)),
                  pl.BlockSpec((t, w_a), lambda b, i: (b, 0)),
                  pl.BlockSpec((tq, LANE), lambda b, i: (b * nq + i, 0)),
                  pl.BlockSpec((1, nh, t), lambda b, i: (b, 0, 0))],
        out_specs=pl.BlockSpec((tq, w_a), lambda b, i: (b * nq + i, 0)),
        out_shape=jax.ShapeDtypeStruct((n, w_a), bf16),
        scratch_shapes=[pltpu.VMEM((nh, tq, LANE), f32)] * 4,
        compiler_params=_params(("parallel", "arbitrary"), 40),
        name="fox_attn",
    )(qa, kab, vab, c, ct)


def _gla_kernel(q_ref, k_ref, v_ref, r_ref, fg_ref, wgk_ref, bgk_ref, gout_ref, s0_ref,
                o_ref, st_ref, s_scr, *, nh, dk, dv, chunk, n_chunks, n_seq, t_valid):
    t = pl.program_id(1)

    @pl.when(t == 0)
    def _():
        s_scr[...] = s0_ref[...]

    tri = _tri(chunk)
    row = lax.broadcasted_iota(jnp.int32, (chunk, chunk), 0)
    col = lax.broadcasted_iota(jnp.int32, (chunk, chunk), 1)
    causal = col <= row
    items = [(e, ci) for e in range(n_seq) for ci in range(n_chunks)]
    rows_of = lambda it: slice((it[0] * n_chunks + it[1]) * chunk, (it[0] * n_chunks + it[1] + 1) * chunk)
    heads = [(slice(h * dk, (h + 1) * dk), slice(h * dv, (h + 1) * dv)) for h in range(nh)]

    def stage1(it):
        la = _log_sigmoid(_dot(fg_ref[rows_of(it), :].astype(bf16), wgk_ref[...]) + bgk_ref[...]) / GK_NORM
        if t_valid is not None:
            la = jnp.where(lax.broadcasted_iota(jnp.int32, la.shape, 0) < t_valid, la, 0.0)
        return la

    def stage2(la):
        b = _dot_sel_left(tri, la)
        b_last = b[chunk - 1:chunk, :]
        return jnp.exp(b), jnp.exp(-b), jnp.exp(b_last - b), jnp.exp(b_last)

    def stage3(it, exps):
        e_pos, e_neg, e_rem, e_last = exps
        rows = rows_of(it)
        out = []
        for kl, _ in heads:
            q = q_ref[rows, kl].astype(f32)
            k = k_ref[rows, kl].astype(f32)
            qe = (q * e_pos[:, kl]).astype(bf16)
            ke = (k * e_neg[:, kl]).astype(bf16)
            kd = (k * e_rem[:, kl]).astype(bf16)
            a = lax.dot_general(qe, ke, _NT, preferred_element_type=f32)
            a = jnp.where(causal, a, 0.0).astype(bf16)
            decay = jnp.broadcast_to(e_last[:, kl], (SUBLANE, dk)).T[:, :1]
            out.append((qe, kd, a, decay))
        return out

    def stage4(it, per_head):
        rows = rows_of(it)
        for h, ((_, vl), (qe, kd, a, decay)) in enumerate(zip(heads, per_head)):
            v = v_ref[rows, vl]
            s = s_scr[it[0], h]
            o = _dot(qe, s.astype(bf16)) + _dot(a, v)
            s_scr[it[0], h] = s * decay + lax.dot_general(kd, v, _TN, preferred_element_type=f32)
            gated = _rms(o, gout_ref[...]) * jax.nn.silu(r_ref[rows, vl].astype(f32))
            o_ref[rows, vl] = gated.astype(bf16)

    r1, r2, r3 = {}, {}, {}
    for tick in range(len(items) + 3):
        if 0 <= tick - 3 < len(items):
            stage4(items[tick - 3], r3.pop(tick - 3))
        if 0 <= tick - 2 < len(items):
            r3[tick - 2] = stage3(items[tick - 2], r2.pop(tick - 2))
        if 0 <= tick - 1 < len(items):
            r2[tick - 1] = stage2(r1.pop(tick - 1))
        if tick < len(items):
            r1[tick] = stage1(items[tick])

    @pl.when(t == pl.num_programs(1) - 1)
    def _():
        st_ref[...] = s_scr[...]


def _gla(qb, kb, vb, rb, fg, wgk_pad, bgk, gout, s0, *, batch, tc, n_seq, chunk, t_valid):
    n, wk_b = qb.shape
    wv_b = vb.shape[1]
    _, nh, dk, dv = s0.shape
    nt = n // batch // tc
    assert n_seq == 1 or nt == 1
    rowspec = lambda width: pl.BlockSpec((n_seq * tc, width), lambda b, t: (b * nt + t, 0))
    state = pl.BlockSpec((n_seq, nh, dk, dv), lambda b, t: (b, 0, 0, 0))
    kern = functools.partial(_gla_kernel, nh=nh, dk=dk, dv=dv, chunk=chunk, n_chunks=tc // chunk, n_seq=n_seq,
                             t_valid=t_valid)
    return pl.pallas_call(
        kern,
        grid=(batch // n_seq, nt),
        in_specs=[rowspec(wk_b), rowspec(wk_b), rowspec(wv_b), rowspec(wv_b), rowspec(LANE),
                  _resident((LANE, wk_b)), _resident((1, wk_b)), _resident((1, dv)), state],
        out_specs=[rowspec(wv_b), state],
        out_shape=[jax.ShapeDtypeStruct((n, wv_b), bf16), jax.ShapeDtypeStruct(s0.shape, f32)],
        scratch_shapes=[pltpu.VMEM((n_seq, nh, dk, dv), f32)],
        compiler_params=_params(("parallel", "arbitrary"), 32),
        name="gla",
    )(qb, kb, vb, rb, fg, wgk_pad, bgk, gout, s0)


def _outproj_kernel(oa_ref, ob_ref, x_ref, w_ref, x1_ref, *, w_a):
    mixed = _dot(oa_ref[...], w_ref[:w_a, :]) + _dot(ob_ref[...], w_ref[w_a:, :])
    x1_ref[...] = x_ref[...] + mixed


def _outproj(oa, ob, x, w_out, *, tm):
    n, d = x.shape
    w_a, wv_b = oa.shape[1], ob.shape[1]
    row = lambda width: pl.BlockSpec((tm, width), lambda i: (i, 0))
    return pl.pallas_call(
        functools.partial(_outproj_kernel, w_a=w_a),
        grid=(n // tm,),
        in_specs=[row(w_a), row(wv_b), row(d), _resident((w_a + wv_b, d))],
        out_specs=row(d),
        out_shape=jax.ShapeDtypeStruct((n, d), f32),
        compiler_params=_params(("parallel",), 40),
        name="outproj",
    )(oa, ob, x, w_out)


def _ffn_kernel(x_ref, g_ref, wup_ref, wdn_ref, o_ref, *rest):
    h_scr = rest[-1]

    @pl.when(pl.program_id(1) == 0)
    def _():
        x = x_ref[...]
        h_scr[...] = _rms(x, g_ref[...]).astype(bf16)
        o_ref[...] = x

    w_up, w_down = wup_ref[...].astype(bf16), wdn_ref[...].astype(bf16)
    if len(rest) == 3:
        rest[0][...] = w_up
        rest[1][...] = w_down
    a = _dot(h_scr[...], w_up)
    a = jnp.square(jnp.maximum(a, 0.0)).astype(bf16)
    o_ref[...] += _dot(a, w_down)


def _ffn(x, g, w_up, w_down, *, tm, tf):
    n, d = x.shape
    dff = w_up.shape[1]
    emit_cast = w_up.dtype != bf16
    assert not emit_cast or n == tm
    x_spec = pl.BlockSpec((tm, d), lambda i, f: (i, 0))
    up_spec = pl.BlockSpec((d, tf), lambda i, f: (0, f))
    down_spec = pl.BlockSpec((tf, d), lambda i, f: (f, 0))
    sds = jax.ShapeDtypeStruct
    out = pl.pallas_call(
        _ffn_kernel,
        grid=(n // tm, dff // tf),
        in_specs=[x_spec, _resident((1, d)), up_spec, down_spec],
        out_specs=[x_spec] + ([up_spec, down_spec] if emit_cast else []),
        out_shape=[sds((n, d), f32)] + ([sds(w_up.shape, bf16), sds(w_down.shape, bf16)] if emit_cast else []),
        scratch_shapes=[pltpu.VMEM((tm, d), bf16)],
        compiler_params=_params(("parallel", "arbitrary"), 52),
        name="ffn",
    )(x, g, w_up, w_down)
    return out if emit_cast else out[0]


def _ple_kernel(x_ref, p_ref, wple_ref, gple_ref, ggate_ref, wgate_ref, gfin_ref, y_ref, *, final):
    x = x_ref[...]
    e = _rms(_dot(p_ref[...].astype(bf16), wple_ref[...]), gple_ref[...])
    gate = jax.nn.sigmoid(_dot(_rms(x, ggate_ref[...]).astype(bf16), wgate_ref[...]))
    x = x + gate * e
    y_ref[...] = _rms(x, gfin_ref[...]) if final else x


def _ple(x, p, w_ple, g_ple, g_gate, w_gate, g_final, *, tm, final):
    n, d = x.shape
    pd = p.shape[1]
    row = lambda width: pl.BlockSpec((tm, width), lambda i: (i, 0))
    return pl.pallas_call(
        functools.partial(_ple_kernel, final=final),
        grid=(n // tm,),
        in_specs=[row(d), row(pd), _resident((pd, d)), _resident((1, d)), _resident((1, d)),
                  _resident((d, d)), _resident((1, d))],
        out_specs=row(d),
        out_shape=jax.ShapeDtypeStruct((n, d), f32),
        compiler_params=_params(("parallel",), 40),
        name="ple",
    )(x, p, w_ple, g_ple, g_gate, w_gate, g_final)


def _scan_tokens(x, nh, n_tokens):
    lane = lax.broadcasted_iota(jnp.int32, x.shape, 1)
    shift = nh
    while shift < n_tokens * nh:
        x = x + jnp.where(lane >= shift, pltpu.roll(x, shift, axis=1), 0.0)
        shift *= 2
    return x


def _spread_last_group(x, nh):
    shift = nh
    while shift < LANE:
        x = x + pltpu.roll(x, shift, axis=1)
        shift *= 2
    return x


def _fox_decode_kernel(pt_ref, q_ref, knew_ref, vnew_ref, lfnew_ref, ck_hbm, cv_hbm, clf_hbm, o_ref,
                       kbuf, vbuf, lbuf, ksem, vsem, lsem, k2, v2, knew_scr, vnew_scr,
                       *, nh, hd, ts, n_pages, ppc):
    b = pl.program_id(0)
    prow = lbuf.shape[2]
    page = prow // nh
    hp = ppc // 2
    half = hp * prow
    rows_q = ts * nh

    def page_copies(hbm, buf, sem, rows, bb, slot):
        return [pltpu.make_async_copy(hbm.at[pt_ref[bb, p]], buf.at[slot, pl.ds(p * rows, rows)], sem.at[slot])
                for p in range(n_pages)]

    def all_copies(bb, slot):
        return (page_copies(ck_hbm, kbuf, ksem, prow, bb, slot) + page_copies(cv_hbm, vbuf, vsem, prow, bb, slot)
                + page_copies(clf_hbm, lbuf, lsem, 1, bb, slot))

    @pl.when(b == 0)
    def _():
        for cp in all_copies(0, 0):
            cp.start()
        knew_scr[...] = jnp.zeros(knew_scr.shape, bf16)
        vnew_scr[...] = jnp.zeros(vnew_scr.shape, bf16)

    @pl.when(b + 1 < pl.num_programs(0))
    def _():
        for cp in all_copies(b + 1, (b + 1) % 2):
            cp.start()

    slot = b % 2
    for cp in all_copies(b, slot):
        cp.wait()

    within = _scan_tokens(lbuf[slot], nh, page)
    lane = lax.broadcasted_iota(jnp.int32, (n_pages, LANE), 1)
    totals = jnp.where(lane >= LANE - nh, within[:, prow - LANE:], 0.0)
    pr = lax.broadcasted_iota(jnp.int32, (n_pages, n_pages), 0)
    pc = lax.broadcasted_iota(jnp.int32, (n_pages, n_pages), 1)
    before = _dot_sel_left((pr > pc).astype(bf16), totals)
    cflat = (within + jnp.concatenate([_spread_last_group(before, nh)] * (prow // LANE), axis=1)) * LOG2E
    whole = _spread_last_group(before[n_pages - 1:, :] + totals[n_pages - 1:, :], nh)
    c_new = (_scan_tokens(lfnew_ref[0], nh, LANE // nh) + whole) * LOG2E
    rr = lax.broadcasted_iota(jnp.int32, (rows_q, LANE), 0)
    ll = lax.broadcasted_iota(jnp.int32, (rows_q, LANE), 1)
    cq = jnp.sum(jnp.where(ll == rr, jnp.broadcast_to(c_new, (rows_q, LANE)), 0.0),
                 axis=-1, keepdims=True)
    q = q_ref[0]
    zero = jnp.zeros(q.shape, bf16)
    q2 = jnp.concatenate([jnp.concatenate([q, zero], axis=1), jnp.concatenate([zero, q], axis=1)], axis=0)

    def both(x):
        return jnp.concatenate([x, x], axis=0)

    m = jnp.full((rows_q, 1), -jnp.inf, f32)
    l = jnp.zeros((rows_q, 1), f32)
    acc = jnp.zeros((2 * rows_q, 2 * hd), f32)
    for blk in range(n_pages // ppc):
        row_a = blk * ppc * prow
        k2[blk, :, :hd] = kbuf[slot, row_a:row_a + half].astype(bf16)
        k2[blk, :, hd:] = kbuf[slot, row_a + half:row_a + 2 * half].astype(bf16)
        s = lax.dot_general(q2, k2[blk], _NT, preferred_element_type=f32)
        pg = blk * ppc
        bias = jnp.concatenate(
            [jnp.concatenate([jnp.broadcast_to(cflat[pg + p:pg + p + 1, :], (rows_q, prow)),
                              jnp.broadcast_to(cflat[pg + hp + p:pg + hp + p + 1, :], (rows_q, prow))], axis=0)
             for p in range(hp)], axis=1)
        r = lax.broadcasted_iota(jnp.int32, s.shape, 0)
        n = lax.broadcasted_iota(jnp.int32, s.shape, 1)
        t = jnp.where(n % nh == r % nh, s - bias, NEG)
        rowmax = jnp.max(t, axis=-1, keepdims=True)
        m_new = jnp.maximum(m, jnp.maximum(rowmax[:rows_q], rowmax[rows_q:]) + cq)
        alpha = jnp.exp2(m - m_new)
        p = jnp.exp2(t + both(cq - m_new))
        rowsum = jnp.sum(p, axis=-1, keepdims=True)
        l = alpha * l + rowsum[:rows_q] + rowsum[rows_q:]
        m = m_new
        v2[blk, :, :hd] = vbuf[slot, row_a:row_a + half].astype(bf16)
        v2[blk, :, hd:] = vbuf[slot, row_a + half:row_a + 2 * half].astype(bf16)
        acc = both(alpha) * acc + _dot(p.astype(bf16), v2[blk])

    knew_scr[:rows_q, :] = knew_ref[0]
    vnew_scr[:rows_q, :] = vnew_ref[0]
    s = lax.dot_general(q, knew_scr[...], _NT, preferred_element_type=f32)
    r = lax.broadcasted_iota(jnp.int32, s.shape, 0)
    n = lax.broadcasted_iota(jnp.int32, s.shape, 1)
    valid = jnp.logical_and(n % nh == r % nh, n // nh <= r // nh)
    t = jnp.where(valid, s - c_new, NEG)
    m_new = jnp.maximum(m, jnp.max(t, axis=-1, keepdims=True) + cq)
    alpha = jnp.exp2(m - m_new)
    p = jnp.exp2(t + (cq - m_new))
    l = alpha * l + jnp.sum(p, axis=-1, keepdims=True)
    acc = both(alpha) * acc
    o = acc[:rows_q, :hd] + acc[rows_q:, hd:] + _dot(p.astype(bf16), vnew_scr[...])
    o_ref[0] = o / l


def _fox_decode(page_table, q, knew, vnew, lfnew, cache_k, cache_v, cache_lf, *, nh, ppc):
    bs, rows_q, hd = q.shape
    ts = rows_q // nh
    n_pages = page_table.shape[1]
    prow = cache_k.shape[1]
    assert rows_q <= LANE and hd == LANE and prow % LANE == 0 and LANE % nh == 0
    assert ppc % 2 == 0 and n_pages % ppc == 0
    any_spec = pl.BlockSpec(memory_space=pl.ANY)
    per_b = lambda shape: pl.BlockSpec((1,) + shape, lambda s, pt: (s, 0, 0))
    half = ppc // 2 * prow
    grid_spec = pltpu.PrefetchScalarGridSpec(
        num_scalar_prefetch=1,
        grid=(bs,),
        in_specs=[per_b((rows_q, hd)), per_b((rows_q, hd)), per_b((rows_q, hd)), per_b((1, LANE)),
                  any_spec, any_spec, any_spec],
        out_specs=per_b((rows_q, hd)),
        scratch_shapes=[
            pltpu.VMEM((2, n_pages * prow, hd), f32), pltpu.VMEM((2, n_pages * prow, hd), f32),
            pltpu.VMEM((2, n_pages, prow), f32),
            pltpu.SemaphoreType.DMA((2,)), pltpu.SemaphoreType.DMA((2,)), pltpu.SemaphoreType.DMA((2,)),
            pltpu.VMEM((n_pages // ppc, half, 2 * hd), bf16), pltpu.VMEM((n_pages // ppc, half, 2 * hd), bf16),
            pltpu.VMEM((LANE, hd), bf16), pltpu.VMEM((LANE, hd), bf16),
        ],
    )
    kern = functools.partial(_fox_decode_kernel, nh=nh, hd=hd, ts=ts, n_pages=n_pages, ppc=ppc)
    return pl.pallas_call(
        kern,
        grid_spec=grid_spec,
        out_shape=jax.ShapeDtypeStruct((bs, rows_q, hd), f32),
        compiler_params=_params(("arbitrary",), 56),
        name="fox_decode",
    )(page_table, q, knew, vnew, lfnew, cache_k, cache_v, cache_lf)


INPROJ_ROWS = 256
ATTN_ROWS = 256
GLA_TOKENS = 1024
GLA_SAMPLE_SEQS = 8
GLA_SAMPLE_PAD = 16
DECODE_PAGES = 4
REPACK_COLS = 256
OUTPROJ_ROWS = 512
FFN_ROWS, FFN_COLS = 512, 1024
FFN_CAST_COLS = 512
PLE_ROWS = 512


def _tile(n, pref):
    while n % pref:
        pref //= 2
    return pref


def kernel(x_prompt, x_sample, cache_k, cache_v, cache_logf, state_gla, page_table, p_prompt, p_sample,
           g_mix, w_in, b_f, w_gk2, b_gk, g_gla_out, w_out, g_mlp, w_up, w_down, w_ple, g_ple, g_ple_gate,
           w_ple_gate, g_final):
    batch, seq, d = x_prompt.shape
    bs, ts, _ = x_sample.shape
    depth = g_mix.shape[0]
    _, n_pool, page, nh, hd = cache_k.shape
    _, _, nh_b, dk, dv = state_gla.shape
    w_a, wk_b, wv_b = nh * hd, nh_b * dk, nh_b * dv
    rank = w_gk2.shape[1]
    n_p, n_s = batch * seq, bs * ts
    assert ts <= SUBLANE and ts % GLA_CHUNK != 0 and seq % GLA_CHUNK == 0
    row = lambda a: a.reshape(1, -1)

    xp = x_prompt.reshape(n_p, d)
    xs = x_sample.reshape(n_s, d)
    outs = {k: [] for k in ("kp", "vp", "fp", "sp", "ks", "vs", "fs", "ss")}
    for l in range(depth):
        final = l == depth - 1
        o_f = 3 * w_a
        o_b = o_f + nh
        o_g = o_b + 2 * wk_b + 2 * wv_b
        w_main, w_fg = _repack_w_in(w_in[l], o_f=o_f, o_b=o_b, o_g=o_g, tn=REPACK_COLS)
        bf_pad = jnp.zeros((1, LANE), f32).at[0, :nh].set(b_f[l])
        wgk_pad = jnp.zeros((LANE, wk_b), f32).at[nh:nh + rank].set(w_gk2[l]).astype(bf16)
        w_out_b, w_ple_b, w_gate_b = w_out[l].astype(bf16), w_ple[l].astype(bf16), w_ple_gate[l].astype(bf16)
        inproj = functools.partial(_inproj, g=row(g_mix[l]), w_main=w_main, w_fg=w_fg, nh=nh, hd=hd,
                                   wk_b=wk_b, wv_b=wv_b, scale_a=hd ** -0.5 * LOG2E, scale_b=dk ** -0.5)
        gla = functools.partial(_gla, wgk_pad=wgk_pad, bgk=row(b_gk[l]), gout=row(g_gla_out[l]))

        def tail(x, oa, ob, p, n, mlp_weights):
            x = _outproj(oa, ob, x, w_out_b, tm=_tile(n, OUTPROJ_ROWS))
            if mlp_weights[0].dtype == bf16:
                x = _ffn(x, row(g_mlp[l]), *mlp_weights, tm=_tile(n, FFN_ROWS), tf=FFN_COLS)
            else:
                x, *mlp_weights = _ffn(x, row(g_mlp[l]), *mlp_weights, tm=n, tf=FFN_CAST_COLS)
            x = _ple(x, p.reshape(n, -1), w_ple_b, row(g_ple[l]), row(g_ple_gate[l]), w_gate_b,
                     row(g_final), tm=_tile(n, PLE_ROWS), final=final)
            return x, mlp_weights

        qa, kab, vab, qb, kb, vb, rb, ka, va, fg = inproj(xs, tm=_tile(n_s, INPROJ_ROWS))
        logf, _, _, _ = _fox_prep(fg, bf_pad, batch=1, nh=nh)
        key_rows = lambda a: a.reshape(bs, ts * nh, hd)
        lfnew = jnp.pad(logf.reshape(bs, 1, ts * nh), ((0, 0), (0, 0), (0, LANE - ts * nh)))
        oa = _fox_decode(page_table, key_rows(qa), key_rows(kab), key_rows(vab), lfnew,
                         cache_k[l].reshape(n_pool, page * nh, hd), cache_v[l].reshape(n_pool, page * nh, hd),
                         cache_logf[l].reshape(n_pool, 1, page * nh), nh=nh, ppc=DECODE_PAGES)
        oa = oa.reshape(n_s, w_a).astype(bf16)
        tpad = GLA_SAMPLE_PAD
        flat = lambda a: jnp.pad(a.reshape(bs, ts, -1), ((0, 0), (0, tpad - ts), (0, 0))).reshape(bs * tpad, -1)
        ob, s_s = gla(flat(qb), flat(kb), flat(vb), flat(rb), flat(fg), s0=state_gla[l], batch=bs,
                      tc=tpad, n_seq=_tile(bs, GLA_SAMPLE_SEQS), chunk=tpad, t_valid=ts)
        ob = ob.reshape(bs, tpad, wv_b)[:, :ts].reshape(n_s, wv_b)
        xs, mlp_weights = tail(xs, oa, ob, p_sample[l], n_s, (w_up[l], w_down[l]))
        outs["ks"].append(ka.reshape(bs, ts, nh, hd))
        outs["vs"].append(va.reshape(bs, ts, nh, hd))
        outs["fs"].append(logf.reshape(bs, ts, nh))
        outs["ss"].append(s_s)

        qa, kab, vab, qb, kb, vb, rb, ka, va, fg = inproj(xp, tm=_tile(n_p, INPROJ_ROWS))
        _, logf_t, c, ct = _fox_prep(fg, bf_pad, batch=batch, nh=nh)
        oa = _fox_attn(qa, kab, vab, c, ct, batch=batch, nh=nh, hd=hd, tq=_tile(seq, ATTN_ROWS))
        ob, s_p = gla(qb, kb, vb, rb, fg, s0=jnp.zeros((batch, nh_b, dk, dv), f32), batch=batch,
                      tc=_tile(seq, GLA_TOKENS), n_seq=1, chunk=GLA_CHUNK, t_valid=None)
        xp, _ = tail(xp, oa, ob, p_prompt[l], n_p, mlp_weights)
        outs["kp"].append(ka.reshape(batch, seq, nh, hd))
        outs["vp"].append(va.reshape(batch, seq, nh, hd))
        outs["fp"].append(jnp.swapaxes(logf_t, 1, 2))
        outs["sp"].append(s_p)

    st = lambda key: jnp.stack(outs[key])
    return (xp.reshape(batch, seq, d), xs.reshape(bs, ts, d), st("kp"), st("vp"), st("fp"), st("sp"),
            st("ks"), st("vs"), st("fs"), st("ss"))
```

```python
import functools

import jax
import jax.numpy as jnp
from jax import lax
from jax.experimental import pallas as pl
from jax.experimental.pallas import tpu as pltpu

f32 = jnp.float32
bf16 = jnp.bfloat16

EPS = 1e-6
NEG = -1e30
GK_NORM = 16.0
GLA_CHUNK = 64
LOG2E = 1.4426950408889634

LANE = 128
SUBLANE = 8
MIB = 2**20
V7X_VMEM_MIB = 64

_NT = (((1,), (1,)), ((), ()))
_TN = (((0,), (0,)), ((), ()))


def _params(semantics, vmem_mib):
    assert vmem_mib < V7X_VMEM_MIB
    return pltpu.CompilerParams(dimension_semantics=semantics, vmem_limit_bytes=vmem_mib * MIB)


def _resident(shape):
    return pl.BlockSpec(shape, lambda *_: (0,) * len(shape), pipeline_mode=pl.Buffered(1))


def _rms(x, g):
    return x * lax.rsqrt(jnp.mean(x * x, axis=-1, keepdims=True) + EPS) * g


def _log_sigmoid(x):
    return jnp.minimum(x, 0.0) - jnp.log1p(jnp.exp(-jnp.abs(x)))


def _split3(x):
    hi = x.astype(bf16)
    r = x - hi.astype(f32)
    mid = r.astype(bf16)
    lo = (r - mid.astype(f32)).astype(bf16)
    return hi, mid, lo


def _dot(a, b):
    return jnp.dot(a, b, preferred_element_type=f32)


def _dot_sel_left(sel, x):
    return sum(_dot(sel, t) for t in _split3(x))


def _tri(n):
    r = lax.broadcasted_iota(jnp.int32, (n, n), 0)
    c = lax.broadcasted_iota(jnp.int32, (n, n), 1)
    return (r >= c).astype(bf16)


def _repack_kernel(wt_hbm, main_ref, fg_ref, buf, small, sem, small_sem, *, o_f, o_b, o_g, tn):
    j = pl.program_id(0)
    n_first = o_f // tn

    def rows_copy(jj, slot):
        start = jnp.where(jj < n_first, jj * tn, o_b + (jj - n_first) * tn)
        return pltpu.make_async_copy(wt_hbm.at[pl.ds(pl.multiple_of(start, SUBLANE), tn)], buf.at[slot], sem.at[slot])

    @pl.when(j == 0)
    def _():
        rows_copy(0, 0).start()

    @pl.when(j + 1 < pl.num_programs(0))
    def _():
        rows_copy(j + 1, (j + 1) % 2).start()

    @pl.when(j == 0)
    def _():
        small[...] = jnp.zeros(small.shape, f32)
        n_f, n_g = o_b - o_f, wt_hbm.shape[0] - o_g
        copies = [pltpu.make_async_copy(wt_hbm.at[pl.ds(o_f, n_f)], small.at[pl.ds(0, n_f)], small_sem.at[0]),
                  pltpu.make_async_copy(wt_hbm.at[pl.ds(o_g, n_g)], small.at[pl.ds(n_f, n_g)], small_sem.at[1])]
        for cp in copies:
            cp.start()
        for cp in copies:
            cp.wait()
        fg_ref[...] = small[...].T.astype(bf16)

    rows_copy(j, j % 2).wait()
    main_ref[...] = buf[j % 2].T.astype(bf16)


def _repack_w_in(w, *, o_f, o_b, o_g, tn):
    d, cols = w.shape
    n_main = o_f + o_g - o_b
    assert o_f % tn == 0 and (o_g - o_b) % tn == 0 and o_b % SUBLANE == 0 and (o_b - o_f) + (cols - o_g) <= LANE
    sds = jax.ShapeDtypeStruct
    return pl.pallas_call(
        functools.partial(_repack_kernel, o_f=o_f, o_b=o_b, o_g=o_g, tn=tn),
        grid=(n_main // tn,),
        in_specs=[pl.BlockSpec(memory_space=pl.ANY)],
        out_specs=[pl.BlockSpec((d, tn), lambda j: (0, j)), pl.BlockSpec((d, LANE), lambda j: (0, 0))],
        out_shape=[sds((d, n_main), bf16), sds((d, LANE), bf16)],
        scratch_shapes=[pltpu.VMEM((2, tn, d), f32), pltpu.VMEM((LANE, d), f32),
                        pltpu.SemaphoreType.DMA((2,)), pltpu.SemaphoreType.DMA((2,))],
        compiler_params=_params(("arbitrary",), 32),
        name="repack_w_in",
    )(jnp.swapaxes(w, 0, 1))


def _inproj_kernel(x_ref, g_ref, w_ref, wfg_ref,
                   qa_ref, kab_ref, vab_ref, qb_ref, kb_ref, vb_ref, rb_ref, ka_ref, va_ref, fg_ref,
                   *, w_a, wk_b, wv_b, scale_a, scale_b, chunk):
    h = _rms(x_ref[...], g_ref[...]).astype(bf16)
    col = 0

    def section(width, store):
        nonlocal col
        for j in range(0, width, chunk):
            store(j, _dot(h, w_ref[:, col + j:col + j + chunk]))
        col += width

    def st_qa(j, z):
        qa_ref[:, j:j + chunk] = (z * scale_a).astype(bf16)

    def st_ka(j, z):
        ka_ref[:, j:j + chunk] = z
        kab_ref[:, j:j + chunk] = z.astype(bf16)

    def st_va(j, z):
        va_ref[:, j:j + chunk] = z
        vab_ref[:, j:j + chunk] = z.astype(bf16)

    def st_qb(j, z):
        qb_ref[:, j:j + chunk] = (z * scale_b).astype(bf16)

    def st_kb(j, z):
        kb_ref[:, j:j + chunk] = z.astype(bf16)

    def st_vb(j, z):
        vb_ref[:, j:j + chunk] = z.astype(bf16)

    def st_rb(j, z):
        rb_ref[:, j:j + chunk] = z.astype(bf16)

    section(w_a, st_qa)
    section(w_a, st_ka)
    section(w_a, st_va)
    section(wk_b, st_qb)
    section(wk_b, st_kb)
    section(wv_b, st_vb)
    section(wv_b, st_rb)
    fg_ref[...] = _dot(h, wfg_ref[...])


def _inproj(x, g, w_main, w_fg, *, nh, hd, wk_b, wv_b, scale_a, scale_b, tm):
    n, d = x.shape
    w_a = nh * hd
    row = lambda width: pl.BlockSpec((tm, width), lambda i: (i, 0))
    sds = jax.ShapeDtypeStruct
    kern = functools.partial(_inproj_kernel, w_a=w_a, wk_b=wk_b, wv_b=wv_b,
                             scale_a=scale_a, scale_b=scale_b, chunk=512)
    return pl.pallas_call(
        kern,
        grid=(n // tm,),
        in_specs=[row(d), _resident((1, d)), _resident(w_main.shape), _resident((d, LANE))],
        out_specs=[row(w_a), row(w_a), row(w_a), row(wk_b), row(wk_b), row(wv_b), row(wv_b),
                   row(w_a), row(w_a), row(LANE)],
        out_shape=[sds((n, w_a), bf16), sds((n, w_a), bf16), sds((n, w_a), bf16),
                   sds((n, wk_b), bf16), sds((n, wk_b), bf16), sds((n, wv_b), bf16), sds((n, wv_b), bf16),
                   sds((n, w_a), f32), sds((n, w_a), f32), sds((n, LANE), f32)],
        compiler_params=_params(("parallel",), 52),
        name="inproj",
    )(x, g, w_main, w_fg)


def _fox_prep_kernel(fg_ref, bf_ref, logf_ref, logft_ref, c_ref, ct_ref, *, nh, blk):
    t = fg_ref.shape[0]
    tri = _tri(blk)
    carry = jnp.zeros((1, LANE), f32)
    for b in range(t // blk):
        rows = slice(b * blk, (b + 1) * blk)
        logf = _log_sigmoid(fg_ref[rows, :] + bf_ref[...])
        logf_ref[rows, :] = logf[:, :nh]
        logft_ref[0, :, rows] = logf.T[:nh, :]
        c = _dot_sel_left(tri, logf) + carry
        carry = c[blk - 1:blk, :]
        c2 = c * LOG2E
        c_ref[rows, :] = c2
        ct_ref[0, :, rows] = c2.T[:nh, :]


def _fox_prep(fg, bf_pad, *, batch, nh):
    n = fg.shape[0]
    t = n // batch
    blk = min(t, 256)
    sds = jax.ShapeDtypeStruct
    return pl.pallas_call(
        functools.partial(_fox_prep_kernel, nh=nh, blk=blk),
        grid=(batch,),
        in_specs=[pl.BlockSpec((t, LANE), lambda b: (b, 0)), _resident((1, LANE))],
        out_specs=[pl.BlockSpec((t, nh), lambda b: (b, 0)),
                   pl.BlockSpec((1, nh, t), lambda b: (b, 0, 0)),
                   pl.BlockSpec((t, LANE), lambda b: (b, 0)),
                   pl.BlockSpec((1, nh, t), lambda b: (b, 0, 0))],
        out_shape=[sds((n, nh), f32), sds((batch, nh, t), f32), sds((n, LANE), f32), sds((batch, nh, t), f32)],
        compiler_params=_params(("parallel",), 32),
        name="fox_prep",
    )(fg, bf_pad)


def _fox_attn_kernel(q_ref, k_ref, v_ref, c_ref, ct_ref, o_ref, m_scr, l_scr, acc_scr, cq_scr, *, nh, hd, tq):
    i = pl.program_id(1)
    m_scr[...] = jnp.full(m_scr.shape, -jnp.inf, f32)
    l_scr[...] = jnp.zeros(l_scr.shape, f32)
    acc_scr[...] = jnp.zeros(acc_scr.shape, f32)
    for h in range(nh):
        cq_scr[h] = jnp.broadcast_to(c_ref[:, h:h + 1], (tq, LANE))

    def weights(h, t):
        m, cq = m_scr[h], cq_scr[h]
        m_new = jnp.maximum(m, jnp.max(t, axis=-1, keepdims=True) + cq)
        alpha = jnp.exp2(m - m_new)
        p = jnp.exp2(t + jnp.concatenate([cq - m_new] * (t.shape[1] // LANE), axis=1))
        l_scr[h] = alpha * l_scr[h] + jnp.sum(p, axis=-1, keepdims=True)
        m_scr[h] = m_new
        return p.astype(bf16), alpha

    def block(start, width, diag):
        keys = pl.ds(pl.multiple_of(start, width), width)
        row = lax.broadcasted_iota(jnp.int32, (tq, width), 0)
        col = lax.broadcasted_iota(jnp.int32, (tq, width), 1)
        for h0 in range(0, nh, 2):
            ps, alphas = [], []
            for h in (h0, h0 + 1):
                lanes = slice(h * hd, (h + 1) * hd)
                t = lax.dot_general(q_ref[:, lanes], k_ref[keys, lanes], _NT, preferred_element_type=f32)
                t = t - ct_ref[0, h:h + 1, keys]
                if diag:
                    t = jnp.where(col <= row + (width - tq), t, NEG)
                p, alpha = weights(h, t)
                ps.append(p)
                alphas.append(alpha)
            pv = _dot(jnp.concatenate(ps, axis=0), v_ref[keys, h0 * hd:(h0 + 2) * hd])
            acc_scr[h0] = alphas[0] * acc_scr[h0] + pv[:tq, :hd]
            acc_scr[h0 + 1] = alphas[1] * acc_scr[h0 + 1] + pv[tq:, hd:]

    def off_diagonal_quad(j, carry):
        block(j * 4 * tq, 4 * tq, False)
        return carry

    lax.fori_loop(0, i // 4, off_diagonal_quad, 0)

    @pl.when(i % 4 >= 2)
    def _():
        block((i // 4) * 4 * tq, 2 * tq, False)

    @pl.when(i % 2 == 1)
    def _():
        block((i - 1) * tq, 2 * tq, True)

    @pl.when(i % 2 == 0)
    def _():
        block(i * tq, tq, True)

    for h in range(nh):
        o_ref[:, h * hd:(h + 1) * hd] = (acc_scr[h] / l_scr[h]).astype(bf16)


def _fox_attn(qa, kab, vab, c, ct, *, batch, nh, hd, tq):
    n, w_a = qa.shape
    t = n // batch
    nq = t // tq
    assert hd == LANE and nh % 2 == 0 and tq % LANE == 0
    return pl.pallas_call(
        functools.partial(_fox_attn_kernel, nh=nh, hd=hd, tq=tq),
        grid=(batch, nq),
        in_specs=[pl.BlockSpec((tq, w_a), lambda b, i: (b * nq + i, 0)),
                  pl.BlockSpec((t, w_a), lambda b, i: (b, 0)),
                  pl.BlockSpec((t, w_a), lambda b, i: (b, 0)),
                  pl.BlockSpec((tq, LANE), lambda b, i: (b * nq + i, 0)),
                  pl.BlockSpec((1, nh, t), lambda b, i: (b, 0, 0))],
        out_specs=pl.BlockSpec((tq, w_a), lambda b, i: (b * nq + i, 0)),
        out_shape=jax.ShapeDtypeStruct((n, w_a), bf16),
        scratch_shapes=[pltpu.VMEM((nh, tq, LANE), f32)] * 4,
        compiler_params=_params(("parallel", "arbitrary"), 40),
        name="fox_attn",
    )(qa, kab, vab, c, ct)


def _gla_kernel(q_ref, k_ref, v_ref, r_ref, fg_ref, wgk_ref, bgk_ref, gout_ref, s0_ref,
                o_ref, st_ref, s_scr, *, nh, dk, dv, chunk, n_chunks, n_seq, t_valid):
    t = pl.program_id(1)

    @pl.when(t == 0)
    def _():
        s_scr[...] = s0_ref[...]

    tri = _tri(chunk)
    row = lax.broadcasted_iota(jnp.int32, (chunk, chunk), 0)
    col = lax.broadcasted_iota(jnp.int32, (chunk, chunk), 1)
    causal = col <= row
    items = [(e, ci) for e in range(n_seq) for ci in range(n_chunks)]
    rows_of = lambda it: slice((it[0] * n_chunks + it[1]) * chunk, (it[0] * n_chunks + it[1] + 1) * chunk)
    heads = [(slice(h * dk, (h + 1) * dk), slice(h * dv, (h + 1) * dv)) for h in range(nh)]

    def stage1(it):
        la = _log_sigmoid(_dot(fg_ref[rows_of(it), :].astype(bf16), wgk_ref[...]) + bgk_ref[...]) / GK_NORM
        if t_valid is not None:
            la = jnp.where(lax.broadcasted_iota(jnp.int32, la.shape, 0) < t_valid, la, 0.0)
        return la

    def stage2(la):
        b = _dot_sel_left(tri, la)
        b_last = b[chunk - 1:chunk, :]
        return jnp.exp(b), jnp.exp(-b), jnp.exp(b_last - b), jnp.exp(b_last)

    def stage3(it, exps):
        e_pos, e_neg, e_rem, e_last = exps
        rows = rows_of(it)
        out = []
        for kl, _ in heads:
            q = q_ref[rows, kl].astype(f32)
            k = k_ref[rows, kl].astype(f32)
            qe = (q * e_pos[:, kl]).astype(bf16)
            ke = (k * e_neg[:, kl]).astype(bf16)
            kd = (k * e_rem[:, kl]).astype(bf16)
            a = lax.dot_general(qe, ke, _NT, preferred_element_type=f32)
            a = jnp.where(causal, a, 0.0).astype(bf16)
            decay = jnp.broadcast_to(e_last[:, kl], (SUBLANE, dk)).T[:, :1]
            out.append((qe, kd, a, decay))
        return out

    def stage4(it, per_head):
        rows = rows_of(it)
        for h, ((_, vl), (qe, kd, a, decay)) in enumerate(zip(heads, per_head)):
            v = v_ref[rows, vl]
            s = s_scr[it[0], h]
            o = _dot(qe, s.astype(bf16)) + _dot(a, v)
            s_scr[it[0], h] = s * decay + lax.dot_general(kd, v, _TN, preferred_element_type=f32)
            gated = _rms(o, gout_ref[...]) * jax.nn.silu(r_ref[rows, vl].astype(f32))
            o_ref[rows, vl] = gated.astype(bf16)

    r1, r2, r3 = {}, {}, {}
    for tick in range(len(items) + 3):
        if 0 <= tick - 3 < len(items):
            stage4(items[tick - 3], r3.pop(tick - 3))
        if 0 <= tick - 2 < len(items):
            r3[tick - 2] = stage3(items[tick - 2], r2.pop(tick - 2))
        if 0 <= tick - 1 < len(items):
            r2[tick - 1] = stage2(r1.pop(tick - 1))
        if tick < len(items):
            r1[tick] = stage1(items[tick])

    @pl.when(t == pl.num_programs(1) - 1)
    def _():
        st_ref[...] = s_scr[...]


def _gla(qb, kb, vb, rb, fg, wgk_pad, bgk, gout, s0, *, batch, tc, n_seq, chunk, t_valid):
    n, wk_b = qb.shape
    wv_b = vb.shape[1]
    _, nh, dk, dv = s0.shape
    nt = n // batch // tc
    assert n_seq == 1 or nt == 1
    rowspec = lambda width: pl.BlockSpec((n_seq * tc, width), lambda b, t: (b * nt + t, 0))
    state = pl.BlockSpec((n_seq, nh, dk, dv), lambda b, t: (b, 0, 0, 0))
    kern = functools.partial(_gla_kernel, nh=nh, dk=dk, dv=dv, chunk=chunk, n_chunks=tc // chunk, n_seq=n_seq,
                             t_valid=t_valid)
    return pl.pallas_call(
        kern,
        grid=(batch // n_seq, nt),
        in_specs=[rowspec(wk_b), rowspec(wk_b), rowspec(wv_b), rowspec(wv_b), rowspec(LANE),
                  _resident((LANE, wk_b)), _resident((1, wk_b)), _resident((1, dv)), state],
        out_specs=[rowspec(wv_b), state],
        out_shape=[jax.ShapeDtypeStruct((n, wv_b), bf16), jax.ShapeDtypeStruct(s0.shape, f32)],
        scratch_shapes=[pltpu.VMEM((n_seq, nh, dk, dv), f32)],
        compiler_params=_params(("parallel", "arbitrary"), 32),
        name="gla",
    )(qb, kb, vb, rb, fg, wgk_pad, bgk, gout, s0)


def _outproj_kernel(oa_ref, ob_ref, x_ref, w_ref, x1_ref, *, w_a):
    mixed = _dot(oa_ref[...], w_ref[:w_a, :]) + _dot(ob_ref[...], w_ref[w_a:, :])
    x1_ref[...] = x_ref[...] + mixed


def _outproj(oa, ob, x, w_out, *, tm):
    n, d = x.shape
    w_a, wv_b = oa.shape[1], ob.shape[1]
    row = lambda width: pl.BlockSpec((tm, width), lambda i: (i, 0))
    return pl.pallas_call(
        functools.partial(_outproj_kernel, w_a=w_a),
        grid=(n // tm,),
        in_specs=[row(w_a), row(wv_b), row(d), _resident((w_a + wv_b, d))],
        out_specs=row(d),
        out_shape=jax.ShapeDtypeStruct((n, d), f32),
        compiler_params=_params(("parallel",), 40),
        name="outproj",
    )(oa, ob, x, w_out)


def _ffn_kernel(x_ref, g_ref, wup_ref, wdn_ref, o_ref, *rest):
    h_scr = rest[-1]

    @pl.when(pl.program_id(1) == 0)
    def _():
        x = x_ref[...]
        h_scr[...] = _rms(x, g_ref[...]).astype(bf16)
        o_ref[...] = x

    w_up, w_down = wup_ref[...].astype(bf16), wdn_ref[...].astype(bf16)
    if len(rest) == 3:
        rest[0][...] = w_up
        rest[1][...] = w_down
    a = _dot(h_scr[...], w_up)
    a = jnp.square(jnp.maximum(a, 0.0)).astype(bf16)
    o_ref[...] += _dot(a, w_down)


def _ffn(x, g, w_up, w_down, *, tm, tf):
    n, d = x.shape
    dff = w_up.shape[1]
    emit_cast = w_up.dtype != bf16
    assert not emit_cast or n == tm
    x_spec = pl.BlockSpec((tm, d), lambda i, f: (i, 0))
    up_spec = pl.BlockSpec((d, tf), lambda i, f: (0, f))
    down_spec = pl.BlockSpec((tf, d), lambda i, f: (f, 0))
    sds = jax.ShapeDtypeStruct
    out = pl.pallas_call(
        _ffn_kernel,
        grid=(n // tm, dff // tf),
        in_specs=[x_spec, _resident((1, d)), up_spec, down_spec],
        out_specs=[x_spec] + ([up_spec, down_spec] if emit_cast else []),
        out_shape=[sds((n, d), f32)] + ([sds(w_up.shape, bf16), sds(w_down.shape, bf16)] if emit_cast else []),
        scratch_shapes=[pltpu.VMEM((tm, d), bf16)],
        compiler_params=_params(("parallel", "arbitrary"), 52),
        name="ffn",
    )(x, g, w_up, w_down)
    return out if emit_cast else out[0]


def _ple_kernel(x_ref, p_ref, wple_ref, gple_ref, ggate_ref, wgate_ref, gfin_ref, y_ref, *, final):
    x = x_ref[...]
    e = _rms(_dot(p_ref[...].astype(bf16), wple_ref[...]), gple_ref[...])
    gate = jax.nn.sigmoid(_dot(_rms(x, ggate_ref[...]).astype(bf16), wgate_ref[...]))
    x = x + gate * e
    y_ref[...] = _rms(x, gfin_ref[...]) if final else x


def _ple(x, p, w_ple, g_ple, g_gate, w_gate, g_final, *, tm, final):
    n, d = x.shape
    pd = p.shape[1]
    row = lambda width: pl.BlockSpec((tm, width), lambda i: (i, 0))
    return pl.pallas_call(
        functools.partial(_ple_kernel, final=final),
        grid=(n // tm,),
        in_specs=[row(d), row(pd), _resident((pd, d)), _resident((1, d)), _resident((1, d)),
                  _resident((d, d)), _resident((1, d))],
        out_specs=row(d),
        out_shape=jax.ShapeDtypeStruct((n, d), f32),
        compiler_params=_params(("parallel",), 40),
        name="ple",
    )(x, p, w_ple, g_ple, g_gate, w_gate, g_final)


def _scan_tokens(x, nh, n_tokens):
    lane = lax.broadcasted_iota(jnp.int32, x.shape, 1)
    shift = nh
    while shift < n_tokens * nh:
        x = x + jnp.where(lane >= shift, pltpu.roll(x, shift, axis=1), 0.0)
        shift *= 2
    return x


def _spread_last_group(x, nh):
    shift = nh
    while shift < LANE:
        x = x + pltpu.roll(x, shift, axis=1)
        shift *= 2
    return x


def _fox_decode_kernel(pt_ref, q_ref, knew_ref, vnew_ref, lfnew_ref, ck_hbm, cv_hbm, clf_hbm, o_ref,
                       kbuf, vbuf, lbuf, ksem, vsem, lsem, k2, v2, knew_scr, vnew_scr,
                       *, nh, hd, ts, n_pages, ppc):
    b = pl.program_id(0)
    prow = lbuf.shape[2]
    page = prow // nh
    hp = ppc // 2
    half = hp * prow
    rows_q = ts * nh

    def page_copies(hbm, buf, sem, rows, bb, slot):
        return [pltpu.make_async_copy(hbm.at[pt_ref[bb, p]], buf.at[slot, pl.ds(p * rows, rows)], sem.at[slot])
                for p in range(n_pages)]

    def all_copies(bb, slot):
        return (page_copies(ck_hbm, kbuf, ksem, prow, bb, slot) + page_copies(cv_hbm, vbuf, vsem, prow, bb, slot)
                + page_copies(clf_hbm, lbuf, lsem, 1, bb, slot))

    @pl.when(b == 0)
    def _():
        for cp in all_copies(0, 0):
            cp.start()
        knew_scr[...] = jnp.zeros(knew_scr.shape, bf16)
        vnew_scr[...] = jnp.zeros(vnew_scr.shape, bf16)

    @pl.when(b + 1 < pl.num_programs(0))
    def _():
        for cp in all_copies(b + 1, (b + 1) % 2):
            cp.start()

    slot = b % 2
    for cp in all_copies(b, slot):
        cp.wait()

    within = _scan_tokens(lbuf[slot], nh, page)
    lane = lax.broadcasted_iota(jnp.int32, (n_pages, LANE), 1)
    totals = jnp.where(lane >= LANE - nh, within[:, prow - LANE:], 0.0)
    pr = lax.broadcasted_iota(jnp.int32, (n_pages, n_pages), 0)
    pc = lax.broadcasted_iota(jnp.int32, (n_pages, n_pages), 1)
    before = _dot_sel_left((pr > pc).astype(bf16), totals)
    cflat = (within + jnp.concatenate([_spread_last_group(before, nh)] * (prow // LANE), axis=1)) * LOG2E
    whole = _spread_last_group(before[n_pages - 1:, :] + totals[n_pages - 1:, :], nh)
    c_new = (_scan_tokens(lfnew_ref[0], nh, LANE // nh) + whole) * LOG2E
    rr = lax.broadcasted_iota(jnp.int32, (rows_q, LANE), 0)
    ll = lax.broadcasted_iota(jnp.int32, (rows_q, LANE), 1)
    cq = jnp.sum(jnp.where(ll == rr, jnp.broadcast_to(c_new, (rows_q, LANE)), 0.0),
                 axis=-1, keepdims=True)
    q = q_ref[0]
    zero = jnp.zeros(q.shape, bf16)
    q2 = jnp.concatenate([jnp.concatenate([q, zero], axis=1), jnp.concatenate([zero, q], axis=1)], axis=0)

    def both(x):
        return jnp.concatenate([x, x], axis=0)

    m = jnp.full((rows_q, 1), -jnp.inf, f32)
    l = jnp.zeros((rows_q, 1), f32)
    acc = jnp.zeros((2 * rows_q, 2 * hd), f32)
    for blk in range(n_pages // ppc):
        row_a = blk * ppc * prow
        k2[blk, :, :hd] = kbuf[slot, row_a:row_a + half].astype(bf16)
        k2[blk, :, hd:] = kbuf[slot, row_a + half:row_a + 2 * half].astype(bf16)
        s = lax.dot_general(q2, k2[blk], _NT, preferred_element_type=f32)
        pg = blk * ppc
        bias = jnp.concatenate(
            [jnp.concatenate([jnp.broadcast_to(cflat[pg + p:pg + p + 1, :], (rows_q, prow)),
                              jnp.broadcast_to(cflat[pg + hp + p:pg + hp + p + 1, :], (rows_q, prow))], axis=0)
             for p in range(hp)], axis=1)
        r = lax.broadcasted_iota(jnp.int32, s.shape, 0)
        n = lax.broadcasted_iota(jnp.int32, s.shape, 1)
        t = jnp.where(n % nh == r % nh, s - bias, NEG)
        rowmax = jnp.max(t, axis=-1, keepdims=True)
        m_new = jnp.maximum(m, jnp.maximum(rowmax[:rows_q], rowmax[rows_q:]) + cq)
        alpha = jnp.exp2(m - m_new)
        p = jnp.exp2(t + both(cq - m_new))
        rowsum = jnp.sum(p, axis=-1, keepdims=True)
        l = alpha * l + rowsum[:rows_q] + rowsum[rows_q:]
        m = m_new
        v2[blk, :, :hd] = vbuf[slot, row_a:row_a + half].astype(bf16)
        v2[blk, :, hd:] = vbuf[slot, row_a + half:row_a + 2 * half].astype(bf16)
        acc = both(alpha) * acc + _dot(p.astype(bf16), v2[blk])

    knew_scr[:rows_q, :] = knew_ref[0]
    vnew_scr[:rows_q, :] = vnew_ref[0]
    s = lax.dot_general(q, knew_scr[...], _NT, preferred_element_type=f32)
    r = lax.broadcasted_iota(jnp.int32, s.shape, 0)
    n = lax.broadcasted_iota(jnp.int32, s.shape, 1)
    valid = jnp.logical_and(n % nh == r % nh, n // nh <= r // nh)
    t = jnp.where(valid, s - c_new, NEG)
    m_new = jnp.maximum(m, jnp.max(t, axis=-1, keepdims=True) + cq)
    alpha = jnp.exp2(m - m_new)
    p = jnp.exp2(t + (cq - m_new))
    l = alpha * l + jnp.sum(p, axis=-1, keepdims=True)
    acc = both(alpha) * acc
    o = acc[:rows_q, :hd] + acc[rows_q:, hd:] + _dot(p.astype(bf16), vnew_scr[...])
    o_ref[0] = o / l


def _fox_decode(page_table, q, knew, vnew, lfnew, cache_k, cache_v, cache_lf, *, nh, ppc):
    bs, rows_q, hd = q.shape
    ts = rows_q // nh
    n_pages = page_table.shape[1]
    prow = cache_k.shape[1]
    assert rows_q <= LANE and hd == LANE and prow % LANE == 0 and LANE % nh == 0
    assert ppc % 2 == 0 and n_pages % ppc == 0
    any_spec = pl.BlockSpec(memory_space=pl.ANY)
    per_b = lambda shape: pl.BlockSpec((1,) + shape, lambda s, pt: (s, 0, 0))
    half = ppc // 2 * prow
    grid_spec = pltpu.PrefetchScalarGridSpec(
        num_scalar_prefetch=1,
        grid=(bs,),
        in_specs=[per_b((rows_q, hd)), per_b((rows_q, hd)), per_b((rows_q, hd)), per_b((1, LANE)),
                  any_spec, any_spec, any_spec],
        out_specs=per_b((rows_q, hd)),
        scratch_shapes=[
            pltpu.VMEM((2, n_pages * prow, hd), f32), pltpu.VMEM((2, n_pages * prow, hd), f32),
            pltpu.VMEM((2, n_pages, prow), f32),
            pltpu.SemaphoreType.DMA((2,)), pltpu.SemaphoreType.DMA((2,)), pltpu.SemaphoreType.DMA((2,)),
            pltpu.VMEM((n_pages // ppc, half, 2 * hd), bf16), pltpu.VMEM((n_pages // ppc, half, 2 * hd), bf16),
            pltpu.VMEM((LANE, hd), bf16), pltpu.VMEM((LANE, hd), bf16),
        ],
    )
    kern = functools.partial(_fox_decode_kernel, nh=nh, hd=hd, ts=ts, n_pages=n_pages, ppc=ppc)
    return pl.pallas_call(
        kern,
        grid_spec=grid_spec,
        out_shape=jax.ShapeDtypeStruct((bs, rows_q, hd), f32),
        compiler_params=_params(("arbitrary",), 56),
        name="fox_decode",
    )(page_table, q, knew, vnew, lfnew, cache_k, cache_v, cache_lf)


INPROJ_ROWS = 256
ATTN_ROWS = 256
GLA_TOKENS = 1024
GLA_SAMPLE_SEQS = 8
GLA_SAMPLE_PAD = 16
DECODE_PAGES = 4
REPACK_COLS = 256
OUTPROJ_ROWS = 512
FFN_ROWS, FFN_COLS = 512, 1024
FFN_CAST_COLS = 512
PLE_ROWS = 512


def _tile(n, pref):
    while n % pref:
        pref //= 2
    return pref


def kernel(x_prompt, x_sample, cache_k, cache_v, cache_logf, state_gla, page_table, p_prompt, p_sample,
           g_mix, w_in, b_f, w_gk2, b_gk, g_gla_out, w_out, g_mlp, w_up, w_down, w_ple, g_ple, g_ple_gate,
           w_ple_gate, g_final):
    batch, seq, d = x_prompt.shape
    bs, ts, _ = x_sample.shape
    depth = g_mix.shape[0]
    _, n_pool, page, nh, hd = cache_k.shape
    _, _, nh_b, dk, dv = state_gla.shape
    w_a, wk_b, wv_b = nh * hd, nh_b * dk, nh_b * dv
    rank = w_gk2.shape[1]
    n_p, n_s = batch * seq, bs * ts
    assert ts <= SUBLANE and ts % GLA_CHUNK != 0 and seq % GLA_CHUNK == 0
    row = lambda a: a.reshape(1, -1)

    xp = x_prompt.reshape(n_p, d)
    xs = x_sample.reshape(n_s, d)
    outs = {k: [] for k in ("kp", "vp", "fp", "sp", "ks", "vs", "fs", "ss")}
    for l in range(depth):
        final = l == depth - 1
        o_f = 3 * w_a
        o_b = o_f + nh
        o_g = o_b + 2 * wk_b + 2 * wv_b
        w_main, w_fg = _repack_w_in(w_in[l], o_f=o_f, o_b=o_b, o_g=o_g, tn=REPACK_COLS)
        bf_pad = jnp.zeros((1, LANE), f32).at[0, :nh].set(b_f[l])
        wgk_pad = jnp.zeros((LANE, wk_b), f32).at[nh:nh + rank].set(w_gk2[l]).astype(bf16)
        w_out_b, w_ple_b, w_gate_b = w_out[l].astype(bf16), w_ple[l].astype(bf16), w_ple_gate[l].astype(bf16)
        inproj = functools.partial(_inproj, g=row(g_mix[l]), w_main=w_main, w_fg=w_fg, nh=nh, hd=hd,
                                   wk_b=wk_b, wv_b=wv_b, scale_a=hd ** -0.5 * LOG2E, scale_b=dk ** -0.5)
        gla = functools.partial(_gla, wgk_pad=wgk_pad, bgk=row(b_gk[l]), gout=row(g_gla_out[l]))

        def tail(x, oa, ob, p, n, mlp_weights):
            x = _outproj(oa, ob, x, w_out_b, tm=_tile(n, OUTPROJ_ROWS))
            if mlp_weights[0].dtype == bf16:
                x = _ffn(x, row(g_mlp[l]), *mlp_weights, tm=_tile(n, FFN_ROWS), tf=FFN_COLS)
            else:
                x, *mlp_weights = _ffn(x, row(g_mlp[l]), *mlp_weights, tm=n, tf=FFN_CAST_COLS)
            x = _ple(x, p.reshape(n, -1), w_ple_b, row(g_ple[l]), row(g_ple_gate[l]), w_gate_b,
                     row(g_final), tm=_tile(n, PLE_ROWS), final=final)
            return x, mlp_weights

        qa, kab, vab, qb, kb, vb, rb, ka, va, fg = inproj(xs, tm=_tile(n_s, INPROJ_ROWS))
        logf, _, _, _ = _fox_prep(fg, bf_pad, batch=1, nh=nh)
        key_rows = lambda a: a.reshape(bs, ts * nh, hd)
        lfnew = jnp.pad(logf.reshape(bs, 1, ts * nh), ((0, 0), (0, 0), (0, LANE - ts * nh)))
        oa = _fox_decode(page_table, key_rows(qa), key_rows(kab), key_rows(vab), lfnew,
                         cache_k[l].reshape(n_pool, page * nh, hd), cache_v[l].reshape(n_pool, page * nh, hd),
                         cache_logf[l].reshape(n_pool, 1, page * nh), nh=nh, ppc=DECODE_PAGES)
        oa = oa.reshape(n_s, w_a).astype(bf16)
        tpad = GLA_SAMPLE_PAD
        flat = lambda a: jnp.pad(a.reshape(bs, ts, -1), ((0, 0), (0, tpad - ts), (0, 0))).reshape(bs * tpad, -1)
        ob, s_s = gla(flat(qb), flat(kb), flat(vb), flat(rb), flat(fg), s0=state_gla[l], batch=bs,
                      tc=tpad, n_seq=_tile(bs, GLA_SAMPLE_SEQS), chunk=tpad, t_valid=ts)
        ob = ob.reshape(bs, tpad, wv_b)[:, :ts].reshape(n_s, wv_b)
        xs, mlp_weights = tail(xs, oa, ob, p_sample[l], n_s, (w_up[l], w_down[l]))
        outs["ks"].append(ka.reshape(bs, ts, nh, hd))
        outs["vs"].append(va.reshape(bs, ts, nh, hd))
        outs["fs"].append(logf.reshape(bs, ts, nh))
        outs["ss"].append(s_s)

        qa, kab, vab, qb, kb, vb, rb, ka, va, fg = inproj(xp, tm=_tile(n_p, INPROJ_ROWS))
        _, logf_t, c, ct = _fox_prep(fg, bf_pad, batch=batch, nh=nh)
        oa = _fox_attn(qa, kab, vab, c, ct, batch=batch, nh=nh, hd=hd, tq=_tile(seq, ATTN_ROWS))
        ob, s_p = gla(qb, kb, vb, rb, fg, s0=jnp.zeros((batch, nh_b, dk, dv), f32), batch=batch,
                      tc=_tile(seq, GLA_TOKENS), n_seq=1, chunk=GLA_CHUNK, t_valid=None)
        xp, _ = tail(xp, oa, ob, p_prompt[l], n_p, mlp_weights)
        outs["kp"].append(ka.reshape(batch, seq, nh, hd))
        outs["vp"].append(va.reshape(batch, seq, nh, hd))
        outs["fp"].append(jnp.swapaxes(logf_t, 1, 2))
        outs["sp"].append(s_p)

    st = lambda key: jnp.stack(outs[key])
    return (xp.reshape(batch, seq, d), xs.reshape(bs, ts, d), st("kp"), st("vp"), st("fp"), st("sp"),
            st("ks"), st("vs"), st("fs"), st("ss"))
```

```python
import functools

import jax
import jax.numpy as jnp
from jax import lax
from jax.experimental import pallas as pl
from jax.experimental.pallas import tpu as pltpu

f32 = jnp.float32
bf16 = jnp.bfloat16

EPS = 1e-6
NEG = -1e30
GK_NORM = 16.0
GLA_CHUNK = 64
LOG2E = 1.4426950408889634

LANE = 128
SUBLANE = 8
MIB = 2**20
V7X_VMEM_MIB = 64

_NT = (((1,), (1,)), ((), ()))
_TN = (((0,), (0,)), ((), ()))


def _params(semantics, vmem_mib):
    assert vmem_mib < V7X_VMEM_MIB
    return pltpu.CompilerParams(dimension_semantics=semantics, vmem_limit_bytes=vmem_mib * MIB)


def _resident(shape):
    return pl.BlockSpec(shape, lambda *_: (0,) * len(shape), pipeline_mode=pl.Buffered(1))


def _rms(x, g):
    return x * lax.rsqrt(jnp.mean(x * x, axis=-1, keepdims=True) + EPS) * g


def _log_sigmoid(x):
    return jnp.minimum(x, 0.0) - jnp.log1p(jnp.exp(-jnp.abs(x)))


def _split3(x):
    hi = x.astype(bf16)
    r = x - hi.astype(f32)
    mid = r.astype(bf16)
    lo = (r - mid.astype(f32)).astype(bf16)
    return hi, mid, lo


def _dot(a, b):
    return jnp.dot(a, b, preferred_element_type=f32)


def _dot_sel_left(sel, x):
    return sum(_dot(sel, t) for t in _split3(x))


def _tri(n):
    r = lax.broadcasted_iota(jnp.int32, (n, n), 0)
    c = lax.broadcasted_iota(jnp.int32, (n, n), 1)
    return (r >= c).astype(bf16)


def _repack_kernel(wt_hbm, main_ref, fg_ref, buf, small, sem, small_sem, *, o_f, o_b, o_g, tn):
    j = pl.program_id(0)
    n_first = o_f // tn

    def rows_copy(jj, slot):
        start = jnp.where(jj < n_first, jj * tn, o_b + (jj - n_first) * tn)
        return pltpu.make_async_copy(wt_hbm.at[pl.ds(pl.multiple_of(start, SUBLANE), tn)], buf.at[slot], sem.at[slot])

    @pl.when(j == 0)
    def _():
        rows_copy(0, 0).start()

    @pl.when(j + 1 < pl.num_programs(0))
    def _():
        rows_copy(j + 1, (j + 1) % 2).start()

    @pl.when(j == 0)
    def _():
        small[...] = jnp.zeros(small.shape, f32)
        n_f, n_g = o_b - o_f, wt_hbm.shape[0] - o_g
        copies = [pltpu.make_async_copy(wt_hbm.at[pl.ds(o_f, n_f)], small.at[pl.ds(0, n_f)], small_sem.at[0]),
                  pltpu.make_async_copy(wt_hbm.at[pl.ds(o_g, n_g)], small.at[pl.ds(n_f, n_g)], small_sem.at[1])]
        for cp in copies:
            cp.start()
        for cp in copies:
            cp.wait()
        fg_ref[...] = small[...].T.astype(bf16)

    rows_copy(j, j % 2).wait()
    main_ref[...] = buf[j % 2].T.astype(bf16)


def _repack_w_in(w, *, o_f, o_b, o_g, tn):
    d, cols = w.shape
    n_main = o_f + o_g - o_b
    assert o_f % tn == 0 and (o_g - o_b) % tn == 0 and o_b % SUBLANE == 0 and (o_b - o_f) + (cols - o_g) <= LANE
    sds = jax.ShapeDtypeStruct
    return pl.pallas_call(
        functools.partial(_repack_kernel, o_f=o_f, o_b=o_b, o_g=o_g, tn=tn),
        grid=(n_main // tn,),
        in_specs=[pl.BlockSpec(memory_space=pl.ANY)],
        out_specs=[pl.BlockSpec((d, tn), lambda j: (0, j)), pl.BlockSpec((d, LANE), lambda j: (0, 0))],
        out_shape=[sds((d, n_main), bf16), sds((d, LANE), bf16)],
        scratch_shapes=[pltpu.VMEM((2, tn, d), f32), pltpu.VMEM((LANE, d), f32),
                        pltpu.SemaphoreType.DMA((2,)), pltpu.SemaphoreType.DMA((2,))],
        compiler_params=_params(("arbitrary",), 32),
        name="repack_w_in",
    )(jnp.swapaxes(w, 0, 1))


def _inproj_kernel(x_ref, g_ref, w_ref, wfg_ref,
                   qa_ref, kab_ref, vab_ref, qb_ref, kb_ref, vb_ref, rb_ref, ka_ref, va_ref, fg_ref,
                   *, w_a, wk_b, wv_b, scale_a, scale_b, chunk):
    h = _rms(x_ref[...], g_ref[...]).astype(bf16)
    col = 0

    def section(width, store):
        nonlocal col
        for j in range(0, width, chunk):
            store(j, _dot(h, w_ref[:, col + j:col + j + chunk]))
        col += width

    def st_qa(j, z):
        qa_ref[:, j:j + chunk] = (z * scale_a).astype(bf16)

    def st_ka(j, z):
        ka_ref[:, j:j + chunk] = z
        kab_ref[:, j:j + chunk] = z.astype(bf16)

    def st_va(j, z):
        va_ref[:, j:j + chunk] = z
        vab_ref[:, j:j + chunk] = z.astype(bf16)

    def st_qb(j, z):
        qb_ref[:, j:j + chunk] = (z * scale_b).astype(bf16)

    def st_kb(j, z):
        kb_ref[:, j:j + chunk] = z.astype(bf16)

    def st_vb(j, z):
        vb_ref[:, j:j + chunk] = z.astype(bf16)

    def st_rb(j, z):
        rb_ref[:, j:j + chunk] = z.astype(bf16)

    section(w_a, st_qa)
    section(w_a, st_ka)
    section(w_a, st_va)
    section(wk_b, st_qb)
    section(wk_b, st_kb)
    section(wv_b, st_vb)
    section(wv_b, st_rb)
    fg_ref[...] = _dot(h, wfg_ref[...])


def _inproj(x, g, w_main, w_fg, *, nh, hd, wk_b, wv_b, scale_a, scale_b, tm):
    n, d = x.shape
    w_a = nh * hd
    row = lambda width: pl.BlockSpec((tm, width), lambda i: (i, 0))
    sds = jax.ShapeDtypeStruct
    kern = functools.partial(_inproj_kernel, w_a=w_a, wk_b=wk_b, wv_b=wv_b,
                             scale_a=scale_a, scale_b=scale_b, chunk=512)
    return pl.pallas_call(
        kern,
        grid=(n // tm,),
        in_specs=[row(d), _resident((1, d)), _resident(w_main.shape), _resident((d, LANE))],
        out_specs=[row(w_a), row(w_a), row(w_a), row(wk_b), row(wk_b), row(wv_b), row(wv_b),
                   row(w_a), row(w_a), row(LANE)],
        out_shape=[sds((n, w_a), bf16), sds((n, w_a), bf16), sds((n, w_a), bf16),
                   sds((n, wk_b), bf16), sds((n, wk_b), bf16), sds((n, wv_b), bf16), sds((n, wv_b), bf16),
                   sds((n, w_a), f32), sds((n, w_a), f32), sds((n, LANE), f32)],
        compiler_params=_params(("parallel",), 52),
        name="inproj",
    )(x, g, w_main, w_fg)


def _fox_prep_kernel(fg_ref, bf_ref, logf_ref, logft_ref, c_ref, ct_ref, *, nh, blk):
    t = fg_ref.shape[0]
    tri = _tri(blk)
    carry = jnp.zeros((1, LANE), f32)
    for b in range(t // blk):
        rows = slice(b * blk, (b + 1) * blk)
        logf = _log_sigmoid(fg_ref[rows, :] + bf_ref[...])
        logf_ref[rows, :] = logf[:, :nh]
        logft_ref[0, :, rows] = logf.T[:nh, :]
        c = _dot_sel_left(tri, logf) + carry
        carry = c[blk - 1:blk, :]
        c2 = c * LOG2E
        c_ref[rows, :] = c2
        ct_ref[0, :, rows] = c2.T[:nh, :]


def _fox_prep(fg, bf_pad, *, batch, nh):
    n = fg.shape[0]
    t = n // batch
    blk = min(t, 256)
    sds = jax.ShapeDtypeStruct
    return pl.pallas_call(
        functools.partial(_fox_prep_kernel, nh=nh, blk=blk),
        grid=(batch,),
        in_specs=[pl.BlockSpec((t, LANE), lambda b: (b, 0)), _resident((1, LANE))],
        out_specs=[pl.BlockSpec((t, nh), lambda b: (b, 0)),
                   pl.BlockSpec((1, nh, t), lambda b: (b, 0, 0)),
                   pl.BlockSpec((t, LANE), lambda b: (b, 0)),
                   pl.BlockSpec((1, nh, t), lambda b: (b, 0, 0))],
        out_shape=[sds((n, nh), f32), sds((batch, nh, t), f32), sds((n, LANE), f32), sds((batch, nh, t), f32)],
        compiler_params=_params(("parallel",), 32),
        name="fox_prep",
    )(fg, bf_pad)


def _fox_attn_kernel(q_ref, k_ref, v_ref, c_ref, ct_ref, o_ref, m_scr, l_scr, acc_scr, cq_scr, *, nh, hd, tq):
    i = pl.program_id(1)
    m_scr[...] = jnp.full(m_scr.shape, -jnp.inf, f32)
    l_scr[...] = jnp.zeros(l_scr.shape, f32)
    acc_scr[...] = jnp.zeros(acc_scr.shape, f32)
    for h in range(nh):
        cq_scr[h] = jnp.broadcast_to(c_ref[:, h:h + 1], (tq, LANE))

    def weights(h, t):
        m, cq = m_scr[h], cq_scr[h]
        m_new = jnp.maximum(m, jnp.max(t, axis=-1, keepdims=True) + cq)
        alpha = jnp.exp2(m - m_new)
        p = jnp.exp2(t + jnp.concatenate([cq - m_new] * (t.shape[1] // LANE), axis=1))
        l_scr[h] = alpha * l_scr[h] + jnp.sum(p, axis=-1, keepdims=True)
        m_scr[h] = m_new
        return p.astype(bf16), alpha

    def block(start, width, diag):
        keys = pl.ds(pl.multiple_of(start, width), width)
        row = lax.broadcasted_iota(jnp.int32, (tq, width), 0)
        col = lax.broadcasted_iota(jnp.int32, (tq, width), 1)
        for h0 in range(0, nh, 2):
            ps, alphas = [], []
            for h in (h0, h0 + 1):
                lanes = slice(h * hd, (h + 1) * hd)
                t = lax.dot_general(q_ref[:, lanes], k_ref[keys, lanes], _NT, preferred_element_type=f32)
                t = t - ct_ref[0, h:h + 1, keys]
                if diag:
                    t = jnp.where(col <= row + (width - tq), t, NEG)
                p, alpha = weights(h, t)
                ps.append(p)
                alphas.append(alpha)
            pv = _dot(jnp.concatenate(ps, axis=0), v_ref[keys, h0 * hd:(h0 + 2) * hd])
            acc_scr[h0] = alphas[0] * acc_scr[h0] + pv[:tq, :hd]
            acc_scr[h0 + 1] = alphas[1] * acc_scr[h0 + 1] + pv[tq:, hd:]

    def off_diagonal_quad(j, carry):
        block(j * 4 * tq, 4 * tq, False)
        return carry

    lax.fori_loop(0, i // 4, off_diagonal_quad, 0)

    @pl.when(i % 4 >= 2)
    def _():
        block((i // 4) * 4 * tq, 2 * tq, False)

    @pl.when(i % 2 == 1)
    def _():
        block((i - 1) * tq, 2 * tq, True)

    @pl.when(i % 2 == 0)
    def _():
        block(i * tq, tq, True)

    for h in range(nh):
        o_ref[:, h * hd:(h + 1) * hd] = (acc_scr[h] / l_scr[h]).astype(bf16)


def _fox_attn(qa, kab, vab, c, ct, *, batch, nh, hd, tq):
    n, w_a = qa.shape
    t = n // batch
    nq = t // tq
    assert hd == LANE and nh % 2 == 0 and tq % LANE == 0
    return pl.pallas_call(
        functools.partial(_fox_attn_kernel, nh=nh, hd=hd, tq=tq),
        grid=(batch, nq),
        in_specs=[pl.BlockSpec((tq, w_a), lambda b, i: (b * nq + i, 0)),
                  pl.BlockSpec((t, w_a), lambda b, i: (b, 0)),
                  pl.BlockSpec((t, w_a), lambda b, i: (b, 0)),
                  pl.BlockSpec((tq, LANE), lambda b, i: (b * nq + i, 0)),
                  pl.BlockSpec((1, nh, t), lambda b, i: (b, 0, 0))],
        out_specs=pl.BlockSpec((tq, w_a), lambda b, i: (b * nq + i, 0)),
        out_shape=jax.ShapeDtypeStruct((n, w_a), bf16),
        scratch_shapes=[pltpu.VMEM((nh, tq, LANE), f32)] * 4,
        compiler_params=_params(("parallel", "arbitrary"), 40),
        name="fox_attn",
    )(qa, kab, vab, c, ct)


def _gla_kernel(q_ref, k_ref, v_ref, r_ref, fg_ref, wgk_ref, bgk_ref, gout_ref, s0_ref,
                o_ref, st_ref, s_scr, *, nh, dk, dv, chunk, n_chunks, n_seq, t_valid):
    t = pl.program_id(1)

    @pl.when(t == 0)
    def _():
        s_scr[...] = s0_ref[...]

    tri = _tri(chunk)
    row = lax.broadcasted_iota(jnp.int32, (chunk, chunk), 0)
    col = lax.broadcasted_iota(jnp.int32, (chunk, chunk), 1)
    causal = col <= row
    items = [(e, ci) for e in range(n_seq) for ci in range(n_chunks)]
    rows_of = lambda it: slice((it[0] * n_chunks + it[1]) * chunk, (it[0] * n_chunks + it[1] + 1) * chunk)
    heads = [(slice(h * dk, (h + 1) * dk), slice(h * dv, (h + 1) * dv)) for h in range(nh)]

    def stage1(it):
        la = _log_sigmoid(_dot(fg_ref[rows_of(it), :].astype(bf16), wgk_ref[...]) + bgk_ref[...]) / GK_NORM
        if t_valid is not None:
            la = jnp.where(lax.broadcasted_iota(jnp.int32, la.shape, 0) < t_valid, la, 0.0)
        return la

    def stage2(la):
        b = _dot_sel_left(tri, la)
        b_last = b[chunk - 1:chunk, :]
        return jnp.exp(b), jnp.exp(-b), jnp.exp(b_last - b), jnp.exp(b_last)

    def stage3(it, exps):
        e_pos, e_neg, e_rem, e_last = exps
        rows = rows_of(it)
        out = []
        for kl, _ in heads:
            q = q_ref[rows, kl].astype(f32)
            k = k_ref[rows, kl].astype(f32)
            qe = (q * e_pos[:, kl]).astype(bf16)
            ke = (k * e_neg[:, kl]).astype(bf16)
            kd = (k * e_rem[:, kl]).astype(bf16)
            a = lax.dot_general(qe, ke, _NT, preferred_element_type=f32)
            a = jnp.where(causal, a, 0.0).astype(bf16)
            decay = jnp.broadcast_to(e_last[:, kl], (SUBLANE, dk)).T[:, :1]
            out.append((qe, kd, a, decay))
        return out

    def stage4(it, per_head):
        rows = rows_of(it)
        for h, ((_, vl), (qe, kd, a, decay)) in enumerate(zip(heads, per_head)):
            v = v_ref[rows, vl]
            s = s_scr[it[0], h]
            o = _dot(qe, s.astype(bf16)) + _dot(a, v)
            s_scr[it[0], h] = s * decay + lax.dot_general(kd, v, _TN, preferred_element_type=f32)
            gated = _rms(o, gout_ref[...]) * jax.nn.silu(r_ref[rows, vl].astype(f32))
            o_ref[rows, vl] = gated.astype(bf16)

    r1, r2, r3 = {}, {}, {}
    for tick in range(len(items) + 3):
        if 0 <= tick - 3 < len(items):
            stage4(items[tick - 3], r3.pop(tick - 3))
        if 0 <= tick - 2 < len(items):
            r3[tick - 2] = stage3(items[tick - 2], r2.pop(tick - 2))
        if 0 <= tick - 1 < len(items):
            r2[tick - 1] = stage2(r1.pop(tick - 1))
        if tick < len(items):
            r1[tick] = stage1(items[tick])

    @pl.when(t == pl.num_programs(1) - 1)
    def _():
        st_ref[...] = s_scr[...]


def _gla(qb, kb, vb, rb, fg, wgk_pad, bgk, gout, s0, *, batch, tc, n_seq, chunk, t_valid):
    n, wk_b = qb.shape
    wv_b = vb.shape[1]
    _, nh, dk, dv = s0.shape
    nt = n // batch // tc
    assert n_seq == 1 or nt == 1
    rowspec = lambda width: pl.BlockSpec((n_seq * tc, width), lambda b, t: (b * nt + t, 0))
    state = pl.BlockSpec((n_seq, nh, dk, dv), lambda b, t: (b, 0, 0, 0))
    kern = functools.partial(_gla_kernel, nh=nh, dk=dk, dv=dv, chunk=chunk, n_chunks=tc // chunk, n_seq=n_seq,
                             t_valid=t_valid)
    return pl.pallas_call(
        kern,
        grid=(batch // n_seq, nt),
        in_specs=[rowspec(wk_b), rowspec(wk_b), rowspec(wv_b), rowspec(wv_b), rowspec(LANE),
                  _resident((LANE, wk_b)), _resident((1, wk_b)), _resident((1, dv)), state],
        out_specs=[rowspec(wv_b), state],
        out_shape=[jax.ShapeDtypeStruct((n, wv_b), bf16), jax.ShapeDtypeStruct(s0.shape, f32)],
        scratch_shapes=[pltpu.VMEM((n_seq, nh, dk, dv), f32)],
        compiler_params=_params(("parallel", "arbitrary"), 32),
        name="gla",
    )(qb, kb, vb, rb, fg, wgk_pad, bgk, gout, s0)


def _outproj_kernel(oa_ref, ob_ref, x_ref, w_ref, x1_ref, *, w_a):
    mixed = _dot(oa_ref[...], w_ref[:w_a, :]) + _dot(ob_ref[...], w_ref[w_a:, :])
    x1_ref[...] = x_ref[...] + mixed


def _outproj(oa, ob, x, w_out, *, tm):
    n, d = x.shape
    w_a, wv_b = oa.shape[1], ob.shape[1]
    row = lambda width: pl.BlockSpec((tm, width), lambda i: (i, 0))
    return pl.pallas_call(
        functools.partial(_outproj_kernel, w_a=w_a),
        grid=(n // tm,),
        in_specs=[row(w_a), row(wv_b), row(d), _resident((w_a + wv_b, d))],
        out_specs=row(d),
        out_shape=jax.ShapeDtypeStruct((n, d), f32),
        compiler_params=_params(("parallel",), 40),
        name="outproj",
    )(oa, ob, x, w_out)


def _ffn_kernel(x_ref, g_ref, wup_ref, wdn_ref, o_ref, *rest):
    h_scr = rest[-1]

    @pl.when(pl.program_id(1) == 0)
    def _():
        x = x_ref[...]
        h_scr[...] = _rms(x, g_ref[...]).astype(bf16)
        o_ref[...] = x

    w_up, w_down = wup_ref[...].astype(bf16), wdn_ref[...].astype(bf16)
    if len(rest) == 3:
        rest[0][...] = w_up
        rest[1][...] = w_down
    a = _dot(h_scr[...], w_up)
    a = jnp.square(jnp.maximum(a, 0.0)).astype(bf16)
    o_ref[...] += _dot(a, w_down)


def _ffn(x, g, w_up, w_down, *, tm, tf):
    n, d = x.shape
    dff = w_up.shape[1]
    emit_cast = w_up.dtype != bf16
    assert not emit_cast or n == tm
    x_spec = pl.BlockSpec((tm, d), lambda i, f: (i, 0))
    up_spec = pl.BlockSpec((d, tf), lambda i, f: (0, f))
    down_spec = pl.BlockSpec((tf, d), lambda i, f: (f, 0))
    sds = jax.ShapeDtypeStruct
    out = pl.pallas_call(
        _ffn_kernel,
        grid=(n // tm, dff // tf),
        in_specs=[x_spec, _resident((1, d)), up_spec, down_spec],
        out_specs=[x_spec] + ([up_spec, down_spec] if emit_cast else []),
        out_shape=[sds((n, d), f32)] + ([sds(w_up.shape, bf16), sds(w_down.shape, bf16)] if emit_cast else []),
        scratch_shapes=[pltpu.VMEM((tm, d), bf16)],
        compiler_params=_params(("parallel", "arbitrary"), 52),
        name="ffn",
    )(x, g, w_up, w_down)
    return out if emit_cast else out[0]


def _ple_kernel(x_ref, p_ref, wple_ref, gple_ref, ggate_ref, wgate_ref, gfin_ref, y_ref, *, final):
    x = x_ref[...]
    e = _rms(_dot(p_ref[...].astype(bf16), wple_ref[...]), gple_ref[...])
    gate = jax.nn.sigmoid(_dot(_rms(x, ggate_ref[...]).astype(bf16), wgate_ref[...]))
    x = x + gate * e
    y_ref[...] = _rms(x, gfin_ref[...]) if final else x


def _ple(x, p, w_ple, g_ple, g_gate, w_gate, g_final, *, tm, final):
    n, d = x.shape
    pd = p.shape[1]
    row = lambda width: pl.BlockSpec((tm, width), lambda i: (i, 0))
    return pl.pallas_call(
        functools.partial(_ple_kernel, final=final),
        grid=(n // tm,),
        in_specs=[row(d), row(pd), _resident((pd, d)), _resident((1, d)), _resident((1, d)),
                  _resident((d, d)), _resident((1, d))],
        out_specs=row(d),
        out_shape=jax.ShapeDtypeStruct((n, d), f32),
        compiler_params=_params(("parallel",), 40),
        name="ple",
    )(x, p, w_ple, g_ple, g_gate, w_gate, g_final)


def _scan_tokens(x, nh, n_tokens):
    lane = lax.broadcasted_iota(jnp.int32, x.shape, 1)
    shift = nh
    while shift < n_tokens * nh:
        x = x + jnp.where(lane >= shift, pltpu.roll(x, shift, axis=1), 0.0)
        shift *= 2
    return x


def _spread_last_group(x, nh):
    shift = nh
    while shift < LANE:
        x = x + pltpu.roll(x, shift, axis=1)
        shift *= 2
    return x


def _fox_decode_kernel(pt_ref, q_ref, knew_ref, vnew_ref, lfnew_ref, ck_hbm, cv_hbm, clf_hbm, o_ref,
                       kbuf, vbuf, lbuf, ksem, vsem, lsem, k2, v2, knew_scr, vnew_scr,
                       *, nh, hd, ts, n_pages, ppc):
    b = pl.program_id(0)
    prow = lbuf.shape[2]
    page = prow // nh
    hp = ppc // 2
    half = hp * prow
    rows_q = ts * nh

    def page_copies(hbm, buf, sem, rows, bb, slot):
        return [pltpu.make_async_copy(hbm.at[pt_ref[bb, p]], buf.at[slot, pl.ds(p * rows, rows)], sem.at[slot])
                for p in range(n_pages)]

    def all_copies(bb, slot):
        return (page_copies(ck_hbm, kbuf, ksem, prow, bb, slot) + page_copies(cv_hbm, vbuf, vsem, prow, bb, slot)
                + page_copies(clf_hbm, lbuf, lsem, 1, bb, slot))

    def start_all(bb, slot):
        for cp in page_copies(ck_hbm, kbuf, ksem, prow, bb, slot) + page_copies(cv_hbm, vbuf, vsem, prow, bb, slot):
            cp.start(priority=0)
        for cp in page_copies(clf_hbm, lbuf, lsem, 1, bb, slot):
            cp.start(priority=1)

    @pl.when(b == 0)
    def _():
        start_all(0, 0)
        knew_scr[...] = jnp.zeros(knew_scr.shape, bf16)
        vnew_scr[...] = jnp.zeros(vnew_scr.shape, bf16)

    @pl.when(b + 1 < pl.num_programs(0))
    def _():
        start_all(b + 1, (b + 1) % 2)

    slot = b % 2
    for cp in all_copies(b, slot):
        cp.wait()

    within = _scan_tokens(lbuf[slot], nh, page)
    lane = lax.broadcasted_iota(jnp.int32, (n_pages, LANE), 1)
    totals = jnp.where(lane >= LANE - nh, within[:, prow - LANE:], 0.0)
    pr = lax.broadcasted_iota(jnp.int32, (n_pages, n_pages), 0)
    pc = lax.broadcasted_iota(jnp.int32, (n_pages, n_pages), 1)
    before = _dot_sel_left((pr > pc).astype(bf16), totals)
    cflat = (within + jnp.concatenate([_spread_last_group(before, nh)] * (prow // LANE), axis=1)) * LOG2E
    whole = _spread_last_group(before[n_pages - 1:, :] + totals[n_pages - 1:, :], nh)
    c_new = (_scan_tokens(lfnew_ref[0], nh, LANE // nh) + whole) * LOG2E
    rr = lax.broadcasted_iota(jnp.int32, (rows_q, LANE), 0)
    ll = lax.broadcasted_iota(jnp.int32, (rows_q, LANE), 1)
    cq = jnp.sum(jnp.where(ll == rr, jnp.broadcast_to(c_new, (rows_q, LANE)), 0.0),
                 axis=-1, keepdims=True)
    q = q_ref[0]
    zero = jnp.zeros(q.shape, bf16)
    q2 = jnp.concatenate([jnp.concatenate([q, zero], axis=1), jnp.concatenate([zero, q], axis=1)], axis=0)

    def both(x):
        return jnp.concatenate([x, x], axis=0)

    m = jnp.full((rows_q, 1), -jnp.inf, f32)
    l = jnp.zeros((rows_q, 1), f32)
    acc = jnp.zeros((2 * rows_q, 2 * hd), f32)
    for blk in range(n_pages // ppc):
        row_a = blk * ppc * prow
        k2[blk, :, :hd] = kbuf[slot, row_a:row_a + half].astype(bf16)
        k2[blk, :, hd:] = kbuf[slot, row_a + half:row_a + 2 * half].astype(bf16)
        s = lax.dot_general(q2, k2[blk], _NT, preferred_element_type=f32)
        pg = blk * ppc
        bias = jnp.concatenate(
            [jnp.concatenate([jnp.broadcast_to(cflat[pg + p:pg + p + 1, :], (rows_q, prow)),
                              jnp.broadcast_to(cflat[pg + hp + p:pg + hp + p + 1, :], (rows_q, prow))], axis=0)
             for p in range(hp)], axis=1)
        r = lax.broadcasted_iota(jnp.int32, s.shape, 0)
        n = lax.broadcasted_iota(jnp.int32, s.shape, 1)
        t = jnp.where(n % nh == r % nh, s - bias, NEG)
        rowmax = jnp.max(t, axis=-1, keepdims=True)
        m_new = jnp.maximum(m, jnp.maximum(rowmax[:rows_q], rowmax[rows_q:]) + cq)
        alpha = jnp.exp2(m - m_new)
        p = jnp.exp2(t + both(cq - m_new))
        rowsum = jnp.sum(p, axis=-1, keepdims=True)
        l = alpha * l + rowsum[:rows_q] + rowsum[rows_q:]
        m = m_new
        v2[blk, :, :hd] = vbuf[slot, row_a:row_a + half].astype(bf16)
        v2[blk, :, hd:] = vbuf[slot, row_a + half:row_a + 2 * half].astype(bf16)
        acc = both(alpha) * acc + _dot(p.astype(bf16), v2[blk])

    knew_scr[:rows_q, :] = knew_ref[0]
    vnew_scr[:rows_q, :] = vnew_ref[0]
    s = lax.dot_general(q, knew_scr[...], _NT, preferred_element_type=f32)
    r = lax.broadcasted_iota(jnp.int32, s.shape, 0)
    n = lax.broadcasted_iota(jnp.int32, s.shape, 1)
    valid = jnp.logical_and(n % nh == r % nh, n // nh <= r // nh)
    t = jnp.where(valid, s - c_new, NEG)
    m_new = jnp.maximum(m, jnp.max(t, axis=-1, keepdims=True) + cq)
    alpha = jnp.exp2(m - m_new)
    p = jnp.exp2(t + (cq - m_new))
    l = alpha * l + jnp.sum(p, axis=-1, keepdims=True)
    acc = both(alpha) * acc
    o = acc[:rows_q, :hd] + acc[rows_q:, hd:] + _dot(p.astype(bf16), vnew_scr[...])
    o_ref[0] = o / l


def _fox_decode(page_table, q, knew, vnew, lfnew, cache_k, cache_v, cache_lf, *, nh, ppc):
    bs, rows_q, hd = q.shape
    ts = rows_q // nh
    n_pages = page_table.shape[1]
    prow = cache_k.shape[1]
    assert rows_q <= LANE and hd == LANE and prow % LANE == 0 and LANE % nh == 0
    assert ppc % 2 == 0 and n_pages % ppc == 0
    any_spec = pl.BlockSpec(memory_space=pl.ANY)
    per_b = lambda shape: pl.BlockSpec((1,) + shape, lambda s, pt: (s, 0, 0))
    half = ppc // 2 * prow
    grid_spec = pltpu.PrefetchScalarGridSpec(
        num_scalar_prefetch=1,
        grid=(bs,),
        in_specs=[per_b((rows_q, hd)), per_b((rows_q, hd)), per_b((rows_q, hd)), per_b((1, LANE)),
                  any_spec, any_spec, any_spec],
        out_specs=per_b((rows_q, hd)),
        scratch_shapes=[
            pltpu.VMEM((2, n_pages * prow, hd), f32), pltpu.VMEM((2, n_pages * prow, hd), f32),
            pltpu.VMEM((2, n_pages, prow), f32),
            pltpu.SemaphoreType.DMA((2,)), pltpu.SemaphoreType.DMA((2,)), pltpu.SemaphoreType.DMA((2,)),
            pltpu.VMEM((n_pages // ppc, half, 2 * hd), bf16), pltpu.VMEM((n_pages // ppc, half, 2 * hd), bf16),
            pltpu.VMEM((LANE, hd), bf16), pltpu.VMEM((LANE, hd), bf16),
        ],
    )
    kern = functools.partial(_fox_decode_kernel, nh=nh, hd=hd, ts=ts, n_pages=n_pages, ppc=ppc)
    return pl.pallas_call(
        kern,
        grid_spec=grid_spec,
        out_shape=jax.ShapeDtypeStruct((bs, rows_q, hd), f32),
        compiler_params=_params(("arbitrary",), 56),
        name="fox_decode",
    )(page_table, q, knew, vnew, lfnew, cache_k, cache_v, cache_lf)


INPROJ_ROWS = 256
ATTN_ROWS = 256
GLA_TOKENS = 1024
GLA_SAMPLE_SEQS = 8
GLA_SAMPLE_PAD = 16
DECODE_PAGES = 4
REPACK_COLS = 256
OUTPROJ_ROWS = 512
FFN_ROWS, FFN_COLS = 512, 1024
FFN_CAST_COLS = 512
PLE_ROWS = 512


def _tile(n, pref):
    while n % pref:
        pref //= 2
    return pref


def kernel(x_prompt, x_sample, cache_k, cache_v, cache_logf, state_gla, page_table, p_prompt, p_sample,
           g_mix, w_in, b_f, w_gk2, b_gk, g_gla_out, w_out, g_mlp, w_up, w_down, w_ple, g_ple, g_ple_gate,
           w_ple_gate, g_final):
    batch, seq, d = x_prompt.shape
    bs, ts, _ = x_sample.shape
    depth = g_mix.shape[0]
    _, n_pool, page, nh, hd = cache_k.shape
    _, _, nh_b, dk, dv = state_gla.shape
    w_a, wk_b, wv_b = nh * hd, nh_b * dk, nh_b * dv
    rank = w_gk2.shape[1]
    n_p, n_s = batch * seq, bs * ts
    assert ts <= SUBLANE and ts % GLA_CHUNK != 0 and seq % GLA_CHUNK == 0
    row = lambda a: a.reshape(1, -1)

    xp = x_prompt.reshape(n_p, d)
    xs = x_sample.reshape(n_s, d)
    outs = {k: [] for k in ("kp", "vp", "fp", "sp", "ks", "vs", "fs", "ss")}
    for l in range(depth):
        final = l == depth - 1
        o_f = 3 * w_a
        o_b = o_f + nh
        o_g = o_b + 2 * wk_b + 2 * wv_b
        w_main, w_fg = _repack_w_in(w_in[l], o_f=o_f, o_b=o_b, o_g=o_g, tn=REPACK_COLS)
        bf_pad = jnp.zeros((1, LANE), f32).at[0, :nh].set(b_f[l])
        wgk_pad = jnp.zeros((LANE, wk_b), f32).at[nh:nh + rank].set(w_gk2[l]).astype(bf16)
        w_out_b, w_ple_b, w_gate_b = w_out[l].astype(bf16), w_ple[l].astype(bf16), w_ple_gate[l].astype(bf16)
        inproj = functools.partial(_inproj, g=row(g_mix[l]), w_main=w_main, w_fg=w_fg, nh=nh, hd=hd,
                                   wk_b=wk_b, wv_b=wv_b, scale_a=hd ** -0.5 * LOG2E, scale_b=dk ** -0.5)
        gla = functools.partial(_gla, wgk_pad=wgk_pad, bgk=row(b_gk[l]), gout=row(g_gla_out[l]))

        def tail(x, oa, ob, p, n, mlp_weights):
            x = _outproj(oa, ob, x, w_out_b, tm=_tile(n, OUTPROJ_ROWS))
            if mlp_weights[0].dtype == bf16:
                x = _ffn(x, row(g_mlp[l]), *mlp_weights, tm=_tile(n, FFN_ROWS), tf=FFN_COLS)
            else:
                x, *mlp_weights = _ffn(x, row(g_mlp[l]), *mlp_weights, tm=n, tf=FFN_CAST_COLS)
            x = _ple(x, p.reshape(n, -1), w_ple_b, row(g_ple[l]), row(g_ple_gate[l]), w_gate_b,
                     row(g_final), tm=_tile(n, PLE_ROWS), final=final)
            return x, mlp_weights

        qa, kab, vab, qb, kb, vb, rb, ka, va, fg = inproj(xs, tm=_tile(n_s, INPROJ_ROWS))
        logf, _, _, _ = _fox_prep(fg, bf_pad, batch=1, nh=nh)
        key_rows = lambda a: a.reshape(bs, ts * nh, hd)
        lfnew = jnp.pad(logf.reshape(bs, 1, ts * nh), ((0, 0), (0, 0), (0, LANE - ts * nh)))
        oa = _fox_decode(page_table, key_rows(qa), key_rows(kab), key_rows(vab), lfnew,
                         cache_k[l].reshape(n_pool, page * nh, hd), cache_v[l].reshape(n_pool, page * nh, hd),
                         cache_logf[l].reshape(n_pool, 1, page * nh), nh=nh, ppc=DECODE_PAGES)
        oa = oa.reshape(n_s, w_a).astype(bf16)
        tpad = GLA_SAMPLE_PAD
        flat = lambda a: jnp.pad(a.reshape(bs, ts, -1), ((0, 0), (0, tpad - ts), (0, 0))).reshape(bs * tpad, -1)
        ob, s_s = gla(flat(qb), flat(kb), flat(vb), flat(rb), flat(fg), s0=state_gla[l], batch=bs,
                      tc=tpad, n_seq=_tile(bs, GLA_SAMPLE_SEQS), chunk=tpad, t_valid=ts)
        ob = ob.reshape(bs, tpad, wv_b)[:, :ts].reshape(n_s, wv_b)
        xs, mlp_weights = tail(xs, oa, ob, p_sample[l], n_s, (w_up[l], w_down[l]))
        outs["ks"].append(ka.reshape(bs, ts, nh, hd))
        outs["vs"].append(va.reshape(bs, ts, nh, hd))
        outs["fs"].append(logf.reshape(bs, ts, nh))
        outs["ss"].append(s_s)

        qa, kab, vab, qb, kb, vb, rb, ka, va, fg = inproj(xp, tm=_tile(n_p, INPROJ_ROWS))
        _, logf_t, c, ct = _fox_prep(fg, bf_pad, batch=batch, nh=nh)
        oa = _fox_attn(qa, kab, vab, c, ct, batch=batch, nh=nh, hd=hd, tq=_tile(seq, ATTN_ROWS))
        ob, s_p = gla(qb, kb, vb, rb, fg, s0=jnp.zeros((batch, nh_b, dk, dv), f32), batch=batch,
                      tc=_tile(seq, GLA_TOKENS), n_seq=1, chunk=GLA_CHUNK, t_valid=None)
        xp, _ = tail(xp, oa, ob, p_prompt[l], n_p, mlp_weights)
        outs["kp"].append(ka.reshape(batch, seq, nh, hd))
        outs["vp"].append(va.reshape(batch, seq, nh, hd))
        outs["fp"].append(jnp.swapaxes(logf_t, 1, 2))
        outs["sp"].append(s_p)

    st = lambda key: jnp.stack(outs[key])
    return (xp.reshape(batch, seq, d), xs.reshape(bs, ts, d), st("kp"), st("vp"), st("fp"), st("sp"),
            st("ks"), st("vs"), st("fs"), st("ss"))
```
